```python
import math, functools
import jax, jax.numpy as jnp
from jax import lax
import numpy as np

D_MODEL = 2048
BATCH = 4
SEQ = 4096
DEPTH = 2

MIX_WIDTH = D_MODEL
GROUP_WIDTH = MIX_WIDTH // 2
ML_HEADS = 4
ML_HEAD_DIM = GROUP_WIDTH // ML_HEADS
RET_HEADS = 4
RET_HEAD_DIM = GROUP_WIDTH // RET_HEADS
SB_HEADS = 16
SB_HEAD_DIM = D_MODEL // SB_HEADS
CHUNK = 128
Q_BLOCK = 128
CONV_WIDTH = 4
FFN_HIDDEN = -(-8 * D_MODEL // (3 * 256)) * 256
IN0_WIDTH = 8 * GROUP_WIDTH + 2 * ML_HEADS
ROPE_BASE = 10000.0
EPS = 1e-6

kernel_name = "hybrid_mlstm_retention_stickbreaking_block"


def rms_norm(x, g):
    xf = x.astype(jnp.float32)
    y = xf * lax.rsqrt(jnp.mean(xf * xf, axis=-1, keepdims=True) + EPS)
    return (y * g.astype(jnp.float32)).astype(x.dtype)


def head_norm(h, g, center):
    H, d = h.shape[-2], h.shape[-1]
    if center:
        h = h - jnp.mean(h, axis=-1, keepdims=True)
    h = h * lax.rsqrt(jnp.mean(h * h, axis=-1, keepdims=True) + EPS)
    return h * g.astype(jnp.float32).reshape(H, d)


def causal_conv(x, w):
    C = x.shape[-1]
    return lax.conv_general_dilated(
        x, w[:, None, :].astype(x.dtype), window_strides=(1,),
        padding=[(w.shape[0] - 1, 0)], dimension_numbers=('NWC', 'WIO', 'NWC'),
        feature_group_count=C)


def rotary(x):
    S, d = x.shape[1], x.shape[-1]
    inv = ROPE_BASE ** (-jnp.arange(0, d, 2, dtype=jnp.float32) / d)
    ang = jnp.arange(S, dtype=jnp.float32)[:, None] * inv[None, :]
    cos = jnp.cos(ang)[None, :, None, :]
    sin = jnp.sin(ang)[None, :, None, :]
    x1, x2 = x[..., : d // 2], x[..., d // 2:]
    return jnp.concatenate([x1 * cos - x2 * sin, x1 * sin + x2 * cos], axis=-1)


def to_chunks(t):
    B, S, H, d = t.shape
    return t.reshape(B, S // CHUNK, CHUNK, H, d).transpose(0, 3, 1, 2, 4)


def from_chunks(t):
    B, H, NC, L, d = t.shape
    return t.transpose(0, 2, 3, 1, 4).reshape(B, NC * L, H, d)


def gate_chunks(t):
    B, S, H = t.shape
    return t.reshape(B, S // CHUNK, CHUNK, H).transpose(0, 3, 1, 2)


def mlstm_chunkwise(q, k, v, i_pre, f_pre):
    d = q.shape[-1]
    q = to_chunks(q.astype(jnp.float32)) * (d ** -0.5)
    k = to_chunks(k.astype(jnp.float32))
    v = to_chunks(v.astype(jnp.float32))
    B, H, NC, L, _ = q.shape
    ig = gate_chunks(i_pre.astype(jnp.float32))
    lf = gate_chunks(jax.nn.log_sigmoid(f_pre.astype(jnp.float32)))
    b = jnp.cumsum(lf, axis=-1)
    g = b[..., -1]
    a = g[..., None] - b + ig
    m_loc = jnp.max(a, axis=-1)
    w_end = jnp.exp(a - m_loc[..., None])
    kv_loc = jnp.einsum('bhcld,bhcle->bhcde', k * w_end[..., None], v)
    n_loc = jnp.einsum('bhcld,bhcl->bhcd', k, w_end)

    def step(carry, inp):
        C, n, m = carry
        g_c, m_l, kv_l, n_l = inp
        m_new = jnp.maximum(g_c + m, m_l)
        s_old = jnp.exp(g_c + m - m_new)
        s_new = jnp.exp(m_l - m_new)
        C_new = s_old[..., None, None] * C + s_new[..., None, None] * kv_l
        n_new = s_old[..., None] * n + s_new[..., None] * n_l
        return (C_new, n_new, m_new), (C, n, m)

    init = (jnp.zeros((B, H, d, d), jnp.float32), jnp.zeros((B, H, d), jnp.float32),
            jnp.zeros((B, H), jnp.float32))
    xs = (jnp.moveaxis(g, 2, 0), jnp.moveaxis(m_loc, 2, 0),
          jnp.moveaxis(kv_loc, 2, 0), jnp.moveaxis(n_loc, 2, 0))
    _, (C0, n0, m0) = lax.scan(step, init, xs)
    C0 = jnp.moveaxis(C0, 0, 2)
    n0 = jnp.moveaxis(n0, 0, 2)
    m0 = jnp.moveaxis(m0, 0, 2)

    causal = jnp.tril(jnp.ones((L, L), dtype=bool))
    log_d = b[..., :, None] - b[..., None, :] + ig[..., None, :]
    log_d = jnp.where(causal, log_d, -jnp.inf)
    m_inter = b + m0[..., None]
    m_t = jnp.maximum(m_inter, jnp.max(log_d, axis=-1))
    w = jnp.einsum('bhcld,bhcsd->bhcls', q, k) * jnp.exp(log_d - m_t[..., None])
    s_inter = jnp.exp(m_inter - m_t)
    num = (jnp.einsum('bhcls,bhcse->bhcle', w, v)
           + s_inter[..., None] * jnp.einsum('bhcld,bhcde->bhcle', q, C0))
    den = jnp.sum(w, axis=-1) + s_inter * jnp.einsum('bhcld,bhcd->bhcl', q, n0)
    h = num / jnp.maximum(jnp.abs(den), jnp.exp(-m_t))[..., None]
    return from_chunks(h)


def retention_chunkwise(q, k, v):
    d = q.shape[-1]
    q = to_chunks(q)
    k = to_chunks(k) * (d ** -0.5)
    v = to_chunks(v)
    B, H, NC, L, _ = q.shape
    log_gamma = jnp.log(1.0 - 2.0 ** (-5.0 - 2.0 * jnp.arange(H, dtype=jnp.float32)))
    pos = jnp.arange(L, dtype=jnp.float32)
    diff = pos[:, None] - pos[None, :]
    causal = diff >= 0
    decay_intra = jnp.where(causal, jnp.exp(jnp.maximum(diff, 0.0)[None] * log_gamma[:, None, None]), 0.0)
    decay_q = jnp.exp((pos + 1.0)[None, :] * log_gamma[:, None])
    decay_k = jnp.exp((L - 1.0 - pos)[None, :] * log_gamma[:, None])
    decay_chunk = jnp.exp(L * log_gamma)

    kv_loc = jnp.einsum('bhcld,bhcle->bhcde', k * decay_k[None, :, None, :, None], v)

    def step(R, kv_l):
        return decay_chunk[None, :, None, None] * R + kv_l, R

    _, R0 = lax.scan(step, jnp.zeros((B, H, d, d), jnp.float32), jnp.moveaxis(kv_loc, 2, 0))
    R0 = jnp.moveaxis(R0, 0, 2)
    scores = jnp.einsum('bhcld,bhcsd->bhcls', q, k) * decay_intra[None, :, None]
    inner = jnp.einsum('bhcls,bhcse->bhcle', scores, v)
    cross = jnp.einsum('bhcld,bhcde->bhcle', q * decay_q[None, :, None, :, None], R0)
    return from_chunks(inner + cross)


def stick_breaking_attention(q, k, v):
    B, S, H, d = q.shape
    q = q.astype(jnp.float32).transpose(0, 2, 1, 3) * (d ** -0.5)
    k = k.astype(jnp.float32).transpose(0, 2, 1, 3)
    v = v.astype(jnp.float32).transpose(0, 2, 1, 3)
    outs = []
    for blk in range(S // Q_BLOCK):
        q0 = blk * Q_BLOCK
        kl = q0 + Q_BLOCK
        z = jnp.einsum('bhtd,bhsd->bhts', q[:, :, q0:kl], k[:, :, :kl])
        t_idx = q0 + jnp.arange(Q_BLOCK)
        s_idx = jnp.arange(kl)
        valid = s_idx[None, :] < t_idx[:, None]
        log_keep = jnp.where(valid, jax.nn.log_sigmoid(-z), 0.0)
        log_remain = lax.cumsum(log_keep, axis=3, reverse=True) - log_keep
        a = jnp.where(valid, jnp.exp(jax.nn.log_sigmoid(z) + log_remain), 0.0)
        outs.append(jnp.einsum('bhts,bhsd->bhtd', a, v[:, :, :kl]))
    o = jnp.concatenate(outs, axis=2)
    return o.transpose(0, 2, 1, 3)


def mlstm_retention_mixer(h, w_in, b_gates, w_conv, g_ml, g_ret, w_out):
    B, S, _ = h.shape
    GW = GROUP_WIDTH
    proj = h @ w_in
    ml_qk, ml_v, ml_o, r_q, r_k, r_v, r_g, gates = jnp.split(
        proj, [2 * GW, 3 * GW, 4 * GW, 5 * GW, 6 * GW, 7 * GW, 8 * GW], axis=-1)
    ml_qk = jax.nn.silu(causal_conv(ml_qk, w_conv))
    ml_q, ml_k = ml_qk[..., :GW], ml_qk[..., GW:]
    gates = gates.astype(jnp.float32) + b_gates.astype(jnp.float32)
    i_pre, f_pre = gates[..., :ML_HEADS], gates[..., ML_HEADS:]
    heads_ml = lambda t: t.reshape(B, S, ML_HEADS, ML_HEAD_DIM)
    heads_ret = lambda t: t.astype(jnp.float32).reshape(B, S, RET_HEADS, RET_HEAD_DIM)
    h_ml = mlstm_chunkwise(heads_ml(ml_q), heads_ml(ml_k), heads_ml(ml_v), i_pre, f_pre)
    h_ml = head_norm(h_ml, g_ml, center=False) * jax.nn.sigmoid(heads_ml(ml_o).astype(jnp.float32))
    h_ret = retention_chunkwise(rotary(heads_ret(r_q)), rotary(heads_ret(r_k)), heads_ret(r_v))
    h_ret = head_norm(h_ret, g_ret, center=True) * jax.nn.silu(heads_ret(r_g))
    y = jnp.concatenate([h_ml.reshape(B, S, GW), h_ret.reshape(B, S, GW)], axis=-1)
    return y.astype(h.dtype) @ w_out


def stick_breaking_mixer(h, w_qkv, w_out):
    B, S, _ = h.shape
    q, k, v = jnp.split(h @ w_qkv, 3, axis=-1)
    heads = lambda t: t.reshape(B, S, SB_HEADS, SB_HEAD_DIM)
    o = stick_breaking_attention(heads(q), heads(k), heads(v))
    return o.reshape(B, S, D_MODEL).astype(h.dtype) @ w_out


def swiglu_ffn(h, w_gu, w_down):
    gate, up = jnp.split(h @ w_gu, 2, axis=-1)
    return (jax.nn.silu(gate) * up) @ w_down


def setup_inputs(seed: int = 0) -> dict:
    key = jax.random.key(seed)
    ks = jax.random.split(key, 24)
    f32 = jnp.float32
    dense = lambda k, fi, fo: jax.random.normal(k, (fi, fo), f32) * (fi ** -0.5)
    gain = lambda k, n: 1.0 + 0.02 * jax.random.normal(k, (n,), f32)
    b_gates0 = jnp.concatenate([
        0.1 * jax.random.normal(ks[3], (ML_HEADS,), f32),
        jnp.linspace(3.0, 6.0, ML_HEADS, dtype=f32) + 0.1 * jax.random.normal(ks[4], (ML_HEADS,), f32),
    ])
    return {
        "x": jax.random.normal(ks[0], (BATCH, SEQ, D_MODEL), f32),
        "norm_mix0": gain(ks[1], D_MODEL),
        "w_in0": dense(ks[2], D_MODEL, IN0_WIDTH),
        "b_gates0": b_gates0,
        "w_conv0": jax.random.normal(ks[5], (CONV_WIDTH, 2 * GROUP_WIDTH), f32) * (CONV_WIDTH ** -0.5),
        "g_ml0": gain(ks[6], GROUP_WIDTH),
        "g_ret0": gain(ks[7], GROUP_WIDTH),
        "w_out0": dense(ks[8], MIX_WIDTH, D_MODEL),
        "norm_ffn0": gain(ks[9], D_MODEL),
        "w_gu0": dense(ks[10], D_MODEL, 2 * FFN_HIDDEN),
        "w_down0": dense(ks[11], FFN_HIDDEN, D_MODEL),
        "norm_mix1": gain(ks[12], D_MODEL),
        "w_qkv1": dense(ks[13], D_MODEL, 3 * D_MODEL),
        "w_out1": dense(ks[14], D_MODEL, D_MODEL),
        "norm_ffn1": gain(ks[15], D_MODEL),
        "w_gu1": dense(ks[16], D_MODEL, 2 * FFN_HIDDEN),
        "w_down1": dense(ks[17], FFN_HIDDEN, D_MODEL),
        "final_norm": gain(ks[18], D_MODEL),
    }


def reference(x, norm_mix0, w_in0, b_gates0, w_conv0, g_ml0, g_ret0, w_out0, norm_ffn0,
              w_gu0, w_down0, norm_mix1, w_qkv1, w_out1, norm_ffn1, w_gu1, w_down1, final_norm):
    mixers = (
        functools.partial(mlstm_retention_mixer, w_in=w_in0, b_gates=b_gates0, w_conv=w_conv0,
                          g_ml=g_ml0, g_ret=g_ret0, w_out=w_out0),
        functools.partial(stick_breaking_mixer, w_qkv=w_qkv1, w_out=w_out1),
    )
    mix_norms = (norm_mix0, norm_mix1)
    ffn_norms = (norm_ffn0, norm_ffn1)
    ffn_weights = ((w_gu0, w_down0), (w_gu1, w_down1))
    for layer in range(DEPTH):
        x = x + mixers[layer](rms_norm(x, mix_norms[layer]))
        x = x + swiglu_ffn(rms_norm(x, ffn_norms[layer]), *ffn_weights[layer])
    return rms_norm(x, final_norm)
```

```python
import functools
import math

import numpy as np
import jax
import jax.numpy as jnp
from jax import lax
from jax.experimental import pallas as pl
from jax.experimental.pallas import tpu as pltpu

F32 = jnp.float32
BF16 = jnp.bfloat16

D_MODEL = 2048
GROUP_WIDTH = 1024
ML_HEADS = 4
RET_HEADS = 4
HEAD_DIM0 = 256
SB_HEADS = 16
SB_HEAD_DIM = 128
CHUNK = 128
CONV_WIDTH = 4
FFN_HIDDEN = 5632
ROPE_BASE = 10000.0
EPS = 1e-6
LANES = 128
SUBLANES = 8
NEG_BIG = -1e30

VMEM_LIMIT = 48 * 1024 * 1024


def _params(semantics):
    return pltpu.CompilerParams(dimension_semantics=semantics, vmem_limit_bytes=VMEM_LIMIT)


def _rms_norm_rows(x, g):
    return x * lax.rsqrt(jnp.mean(x * x, axis=-1, keepdims=True) + EPS) * g


def _dot(a, b):
    return jnp.dot(a, b, preferred_element_type=F32)


def _dot_nt(a, b):
    return lax.dot_general(a, b, (((1,), (1,)), ((), ())), preferred_element_type=F32)


def _norm_proj_kernel(x_ref, g_ref, w_ref, o_ref, xn_ref):
    @pl.when(pl.program_id(1) == 0)
    def _():
        xn_ref[...] = _rms_norm_rows(x_ref[...], g_ref[...]).astype(BF16)

    o_ref[...] = _dot(xn_ref[...], w_ref[...]).astype(o_ref.dtype)


def _norm_proj_gates_kernel(x_ref, g_ref, w_ref, wg_ref, o_ref, gates_ref, xn_ref):
    @pl.when(pl.program_id(1) == 0)
    def _():
        xn = _rms_norm_rows(x_ref[...], g_ref[...]).astype(BF16)
        xn_ref[...] = xn
        gates_ref[...] = _dot(xn, wg_ref[...])

    o_ref[...] = _dot(xn_ref[...], w_ref[...]).astype(o_ref.dtype)


def norm_proj(x, g, w, w_gates=None, *, tm=1024, tn=512):
    m, d = x.shape
    n = w.shape[1]
    grid = (m // tm, n // tn)
    x_spec = pl.BlockSpec((tm, d), lambda i, j: (i, 0))
    g_spec = pl.BlockSpec((1, d), lambda i, j: (0, 0))
    w_spec = pl.BlockSpec((d, tn), lambda i, j: (0, j))
    o_spec = pl.BlockSpec((tm, tn), lambda i, j: (i, j))
    scratch = [pltpu.VMEM((tm, d), BF16)]
    if w_gates is None:
        return pl.pallas_call(
            _norm_proj_kernel,
            grid=grid,
            in_specs=[x_spec, g_spec, w_spec],
            out_specs=o_spec,
            out_shape=jax.ShapeDtypeStruct((m, n), BF16),
            scratch_shapes=scratch,
            compiler_params=_params(("parallel", "arbitrary")),
            name="norm_proj",
        )(x, g, w)
    ng = w_gates.shape[1]
    return pl.pallas_call(
        _norm_proj_gates_kernel,
        grid=grid,
        in_specs=[x_spec, g_spec, w_spec, pl.BlockSpec((d, ng), lambda i, j: (0, 0))],
        out_specs=[o_spec, pl.BlockSpec((tm, ng), lambda i, j: (i, 0))],
        out_shape=[jax.ShapeDtypeStruct((m, n), BF16), jax.ShapeDtypeStruct((m, ng), F32)],
        scratch_shapes=scratch,
        compiler_params=_params(("parallel", "arbitrary")),
        name="norm_proj_gates",
    )(x, g, w, w_gates)


def _proj_residual_kernel(a_ref, w_ref, r_ref, o_ref):
    o_ref[...] = r_ref[...] + _dot(a_ref[...], w_ref[...])


def proj_residual(a, w, r, *, tm=1024, tn=512):
    m, k = a.shape
    n = w.shape[1]
    return pl.pallas_call(
        _proj_residual_kernel,
        grid=(m // tm, n // tn),
        in_specs=[
            pl.BlockSpec((tm, k), lambda i, j: (i, 0)),
            pl.BlockSpec((k, tn), lambda i, j: (0, j)),
            pl.BlockSpec((tm, tn), lambda i, j: (i, j)),
        ],
        out_specs=pl.BlockSpec((tm, tn), lambda i, j: (i, j)),
        out_shape=jax.ShapeDtypeStruct((m, n), F32),
        compiler_params=_params(("parallel", "arbitrary")),
        name="proj_residual",
    )(a, w, r)


def _ffn_kernel(x_ref, g_ref, wg_ref, wu_ref, wd_ref, gf_ref, o_ref, xn_ref, *, final_norm):
    f = pl.program_id(1)

    @pl.when(f == 0)
    def _():
        x = x_ref[...]
        xn_ref[...] = _rms_norm_rows(x, g_ref[...]).astype(BF16)
        o_ref[...] = x

    xn = xn_ref[...]
    gate = _dot(xn, wg_ref[...])
    up = _dot(xn, wu_ref[...])
    act = (gate * jax.nn.sigmoid(gate) * up).astype(BF16)
    o_ref[...] += _dot(act, wd_ref[...])

    if final_norm:
        @pl.when(f == pl.num_programs(1) - 1)
        def _():
            o_ref[...] = _rms_norm_rows(o_ref[...], gf_ref[...])


def ffn_residual(x, g, w_gu, w_down, g_final, *, final_norm, tm=512, tf=512):
    m, d = x.shape
    hidden = w_down.shape[0]
    nf = hidden // tf
    return pl.pallas_call(
        functools.partial(_ffn_kernel, final_norm=final_norm),
        grid=(m // tm, nf),
        in_specs=[
            pl.BlockSpec((tm, d), lambda i, f: (i, 0)),
            pl.BlockSpec((1, d), lambda i, f: (0, 0)),
            pl.BlockSpec((d, tf), lambda i, f: (0, f)),
            pl.BlockSpec((d, tf), lambda i, f: (0, f + nf)),
            pl.BlockSpec((tf, d), lambda i, f: (f, 0)),
            pl.BlockSpec((1, d), lambda i, f: (0, 0)),
        ],
        out_specs=pl.BlockSpec((tm, d), lambda i, f: (i, 0)),
        out_shape=jax.ShapeDtypeStruct((m, d), F32),
        scratch_shapes=[pltpu.VMEM((tm, d), BF16)],
        compiler_params=_params(("parallel", "arbitrary")),
        name="ffn_residual",
    )(x, g, w_gu, w_gu, w_down, g_final)


def _shift_rows(x, tail, s):
    xr = pltpu.roll(x, s, 0)
    tr = pltpu.roll(tail, s, 0)
    row = lax.broadcasted_iota(jnp.int32, tr.shape, 0)
    head = jnp.where(row < s, tr, xr[:SUBLANES])
    return jnp.concatenate([head, xr[SUBLANES:]], axis=0)


def _conv_silu(x, tail, w):
    acc = x * w[CONV_WIDTH - 1:CONV_WIDTH]
    for s in range(1, CONV_WIDTH):
        acc = acc + _shift_rows(x, tail, s) * w[CONV_WIDTH - 1 - s:CONV_WIDTH - s]
    return acc * jax.nn.sigmoid(acc)


def _log_sigmoid(x):
    return jnp.minimum(x, 0.0) - jnp.log1p(jnp.exp(-jnp.abs(x)))


def _mixer0_kernel(proj_ref, gates_ref, bg_ref, wconv_ref, gml_ref, gret_ref, cos_ref, sin_ref,
                   y_ref, c_ref, n_ref, m_ref, r_ref, tail_ref):
    L, HD, GW = CHUNK, HEAD_DIM0, GROUP_WIDTH

    @pl.when(pl.program_id(1) == 0)
    def _():
        c_ref[...] = jnp.zeros_like(c_ref)
        n_ref[...] = jnp.zeros_like(n_ref)
        m_ref[...] = jnp.zeros_like(m_ref)
        r_ref[...] = jnp.zeros_like(r_ref)
        tail_ref[...] = jnp.zeros_like(tail_ref)

    row = lax.broadcasted_iota(jnp.int32, (L, L), 0)
    col = lax.broadcasted_iota(jnp.int32, (L, L), 1)
    causal = col <= row
    diff = (row - col).astype(F32)

    gates = gates_ref[...] + bg_ref[...]
    gates_t = gates.T

    for h in range(ML_HEADS):
        qs = slice(h * HD, (h + 1) * HD)
        ks = slice(GW + h * HD, GW + (h + 1) * HD)
        q_raw = proj_ref[:, qs].astype(F32)
        k_raw = proj_ref[:, ks].astype(F32)
        q = _conv_silu(q_raw, tail_ref[:, qs], wconv_ref[:, qs]) * (HD ** -0.5)
        k = _conv_silu(k_raw, tail_ref[:, ks], wconv_ref[:, ks])
        tail_ref[:, qs] = q_raw[L - SUBLANES:]
        tail_ref[:, ks] = k_raw[L - SUBLANES:]
        v = proj_ref[:, 2 * GW + h * HD:2 * GW + (h + 1) * HD]
        o_gate = proj_ref[:, 3 * GW + h * HD:3 * GW + (h + 1) * HD].astype(F32)

        ig_col = gates[:, h:h + 1]
        lf_col = _log_sigmoid(gates[:, ML_HEADS + h:ML_HEADS + h + 1])
        ig_row = gates_t[h:h + 1, :]
        lf_row = _log_sigmoid(gates_t[ML_HEADS + h:ML_HEADS + h + 1, :])
        b_col = jnp.sum(jnp.where(causal, lf_row, 0.0), axis=1, keepdims=True)
        b_row = jnp.sum(jnp.where(row <= col, lf_col, 0.0), axis=0, keepdims=True)
        g_tot = jnp.sum(lf_row, axis=1, keepdims=True)
        a_col = g_tot - b_col + ig_col
        m_loc = jnp.max(a_col, axis=0, keepdims=True)
        w_end = jnp.exp(a_col - m_loc)

        m_prev = m_ref[h]
        log_d = jnp.where(causal, b_col - b_row + ig_row, NEG_BIG)
        m_inter = b_col + m_prev
        m_t = jnp.maximum(m_inter, jnp.max(log_d, axis=1, keepdims=True))
        q_b = q.astype(BF16)
        k_b = k.astype(BF16)
        w = _dot_nt(q_b, k_b) * jnp.exp(log_d - m_t)
        s_inter = jnp.exp(m_inter - m_t)
        c_prev = c_ref[h]
        n_prev = n_ref[h]
        num = _dot(w.astype(BF16), v) + s_inter * _dot(q_b, c_prev.astype(BF16))
        den = jnp.sum(w, axis=1, keepdims=True) + s_inter * jnp.sum(q * n_prev, axis=1, keepdims=True)
        hid = num / jnp.maximum(jnp.abs(den), jnp.exp(-m_t))

        m_new = jnp.maximum(g_tot + m_prev, m_loc)
        s_old = jnp.exp(g_tot + m_prev - m_new)
        s_new = jnp.exp(m_loc - m_new)
        kw = k * w_end
        c_ref[h] = s_old * c_prev + s_new * _dot(kw.T.astype(BF16), v)
        n_ref[h] = s_old * n_prev + s_new * jnp.sum(kw, axis=0, keepdims=True)
        m_ref[h] = m_new

        hid = hid * lax.rsqrt(jnp.mean(hid * hid, axis=1, keepdims=True) + EPS) * gml_ref[:, qs]
        y_ref[:, qs] = (hid * jax.nn.sigmoid(o_gate)).astype(y_ref.dtype)

    cos = cos_ref[...]
    sin = sin_ref[...]
    half = HD // 2

    def rotary(t):
        t1, t2 = t[:, :half], t[:, half:]
        return jnp.concatenate([t1 * cos - t2 * sin, t1 * sin + t2 * cos], axis=1)

    pos_col = lax.broadcasted_iota(jnp.int32, (L, 1), 0).astype(F32)
    for h in range(RET_HEADS):
        base = 4 * GW + h * HD
        log_gamma = float(np.log(np.float32(1.0) - np.float32(2.0) ** np.float32(-5.0 - 2.0 * h)))
        q = rotary(proj_ref[:, base:base + HD].astype(F32))
        k = rotary(proj_ref[:, base + GW:base + GW + HD].astype(F32)) * (HD ** -0.5)
        v = proj_ref[:, base + 2 * GW:base + 2 * GW + HD]
        r_gate = proj_ref[:, base + 3 * GW:base + 3 * GW + HD].astype(F32)

        decay_intra = jnp.where(causal, jnp.exp(jnp.maximum(diff, 0.0) * log_gamma), 0.0)
        decay_q = jnp.exp((pos_col + 1.0) * log_gamma)
        decay_k = jnp.exp((L - 1.0 - pos_col) * log_gamma)
        decay_chunk = math.exp(L * log_gamma)

        r_prev = r_ref[h]
        k_b = k.astype(BF16)
        scores = _dot_nt(q.astype(BF16), k_b) * decay_intra
        out = _dot(scores.astype(BF16), v) + _dot((q * decay_q).astype(BF16), r_prev.astype(BF16))
        r_ref[h] = decay_chunk * r_prev + _dot((k * decay_k).T.astype(BF16), v)

        out = out - jnp.mean(out, axis=1, keepdims=True)
        out = out * lax.rsqrt(jnp.mean(out * out, axis=1, keepdims=True) + EPS) * gret_ref[:, h * HD:(h + 1) * HD]
        y_ref[:, GW + h * HD:GW + (h + 1) * HD] = (out * (r_gate * jax.nn.sigmoid(r_gate))).astype(y_ref.dtype)


def mixer0(proj, gates, b_gates, w_conv, g_ml, g_ret, cos, sin):
    bsz, seq, width = proj.shape
    hd = HEAD_DIM0
    return pl.pallas_call(
        _mixer0_kernel,
        grid=(bsz, seq // CHUNK),
        in_specs=[
            pl.BlockSpec((None, CHUNK, width), lambda b, c: (b, c, 0)),
            pl.BlockSpec((None, CHUNK, LANES), lambda b, c: (b, c, 0)),
            pl.BlockSpec((1, LANES), lambda b, c: (0, 0)),
            pl.BlockSpec((CONV_WIDTH, 2 * GROUP_WIDTH), lambda b, c: (0, 0)),
            pl.BlockSpec((1, GROUP_WIDTH), lambda b, c: (0, 0)),
            pl.BlockSpec((1, GROUP_WIDTH), lambda b, c: (0, 0)),
            pl.BlockSpec((CHUNK, hd // 2), lambda b, c: (c, 0)),
            pl.BlockSpec((CHUNK, hd // 2), lambda b, c: (c, 0)),
        ],
        out_specs=pl.BlockSpec((None, CHUNK, 2 * GROUP_WIDTH), lambda b, c: (b, c, 0)),
        out_shape=jax.ShapeDtypeStruct((bsz, seq, 2 * GROUP_WIDTH), BF16),
        scratch_shapes=[
            pltpu.VMEM((ML_HEADS, hd, hd), F32),
            pltpu.VMEM((ML_HEADS, 1, hd), F32),
            pltpu.VMEM((ML_HEADS, 1, 1), F32),
            pltpu.VMEM((RET_HEADS, hd, hd), F32),
            pltpu.VMEM((SUBLANES, 2 * GROUP_WIDTH), F32),
        ],
        compiler_params=_params(("parallel", "arbitrary")),
        name="mixer0",
    )(proj, gates, b_gates, w_conv, g_ml, g_ret, cos, sin)


SB_BLOCK = 256


def _sb_kernel(q_ref, k_ref, v_ref, o_ref):
    T = SB_BLOCK
    qi = pl.program_id(2)
    q = (q_ref[...].astype(F32) * (SB_HEAD_DIM ** -0.5)).astype(BF16)

    row = lax.broadcasted_iota(jnp.int32, (T, T), 0)
    col = lax.broadcasted_iota(jnp.int32, (T, T), 1)
    ones_incl = jnp.where(row >= col, 1.0, 0.0).astype(BF16)
    cum_rhs = jnp.concatenate([ones_incl, ones_incl], axis=0)
    strict = col < row

    def block(j, acc, rem, masked):
        start = pl.multiple_of(j * T, T)
        k = k_ref[pl.ds(start, T), :]
        v = v_ref[pl.ds(start, T), :]
        z = _dot_nt(q, k)
        log_keep = -(jnp.maximum(z, 0.0) + jnp.log1p(jnp.exp(-jnp.abs(z))))
        if masked:
            log_keep = jnp.where(strict, log_keep, 0.0)
        hi = log_keep.astype(BF16)
        lo = (log_keep - hi.astype(F32)).astype(BF16)
        cum = _dot(jnp.concatenate([hi, lo], axis=1), cum_rhs)
        a = jnp.exp(z + cum + rem)
        if masked:
            a = jnp.where(strict, a, 0.0)
        acc = acc + _dot(a.astype(BF16), v)
        rem = rem + cum[:, 0:1]
        return acc, rem

    acc0 = jnp.zeros((T, SB_HEAD_DIM), F32)
    rem0 = jnp.zeros((T, 1), F32)
    acc, rem = block(qi, acc0, rem0, True)

    def body(it, carry):
        return block(qi - 1 - it, carry[0], carry[1], False)

    acc, rem = lax.fori_loop(0, qi, body, (acc, rem))
    o_ref[...] = acc.astype(o_ref.dtype)


def stick_breaking(qkv):
    bsz, seq, _ = qkv.shape
    T = SB_BLOCK
    return pl.pallas_call(
        _sb_kernel,
        grid=(bsz, SB_HEADS, seq // T),
        in_specs=[
            pl.BlockSpec((None, T, SB_HEAD_DIM), lambda b, h, i: (b, i, h)),
            pl.BlockSpec((None, seq, SB_HEAD_DIM), lambda b, h, i: (b, 0, SB_HEADS + h)),
            pl.BlockSpec((None, seq, SB_HEAD_DIM), lambda b, h, i: (b, 0, 2 * SB_HEADS + h)),
        ],
        out_specs=pl.BlockSpec((None, T, SB_HEAD_DIM), lambda b, h, i: (b, i, h)),
        out_shape=jax.ShapeDtypeStruct((bsz, seq, D_MODEL), BF16),
        compiler_params=_params(("parallel", "parallel", "arbitrary")),
        name="stick_breaking",
    )(qkv, qkv, qkv)


def _rotary_tables(seq):
    half = HEAD_DIM0 // 2
    inv = ROPE_BASE ** (-jnp.arange(0, HEAD_DIM0, 2, dtype=F32) / HEAD_DIM0)
    ang = jnp.arange(seq, dtype=F32)[:, None] * inv[None, :]
    assert ang.shape == (seq, half)
    return jnp.cos(ang), jnp.sin(ang)


def kernel(x, norm_mix0, w_in0, b_gates0, w_conv0, g_ml0, g_ret0, w_out0, norm_ffn0, w_gu0, w_down0,
           norm_mix1, w_qkv1, w_out1, norm_ffn1, w_gu1, w_down1, final_norm):
    bsz, seq, d = x.shape
    m = bsz * seq
    row = lambda t: t.reshape(1, -1).astype(F32)
    n_main = 8 * GROUP_WIDTH
    n_gates = 2 * ML_HEADS

    xf = x.reshape(m, d)

    w_main = w_in0[:, :n_main].astype(BF16)
    w_gates = jnp.pad(w_in0[:, n_main:], ((0, 0), (0, LANES - n_gates))).astype(BF16)
    b_gates = jnp.pad(b_gates0.astype(F32), (0, LANES - n_gates)).reshape(1, LANES)
    proj, gates = norm_proj(xf, row(norm_mix0), w_main, w_gates)
    cos, sin = _rotary_tables(seq)
    y = mixer0(proj.reshape(bsz, seq, n_main), gates.reshape(bsz, seq, LANES), b_gates,
               w_conv0.astype(F32), row(g_ml0), row(g_ret0), cos, sin)
    xf = proj_residual(y.reshape(m, d), w_out0.astype(BF16), xf)
    xf = ffn_residual(xf, row(norm_ffn0), w_gu0.astype(BF16), w_down0.astype(BF16), row(final_norm),
                      final_norm=False)

    qkv = norm_proj(xf, row(norm_mix1), w_qkv1.astype(BF16))
    o = stick_breaking(qkv.reshape(bsz, seq, 3 * d))
    xf = proj_residual(o.reshape(m, d), w_out1.astype(BF16), xf)
    xf = ffn_residual(xf, row(norm_ffn1), w_gu1.astype(BF16), w_down1.astype(BF16), row(final_norm),
                      final_norm=True)
    return xf.reshape(bsz, seq, d)
```

```python
import functools
import math

import numpy as np
import jax
import jax.numpy as jnp
from jax import lax
from jax.experimental import pallas as pl
from jax.experimental.pallas import tpu as pltpu

F32 = jnp.float32
BF16 = jnp.bfloat16

D_MODEL = 2048
GROUP_WIDTH = 1024
ML_HEADS = 4
RET_HEADS = 4
HEAD_DIM0 = 256
SB_HEADS = 16
SB_HEAD_DIM = 128
CHUNK = 128
CONV_WIDTH = 4
FFN_HIDDEN = 5632
ROPE_BASE = 10000.0
EPS = 1e-6
LANES = 128
SUBLANES = 8
NEG_BIG = -1e30

VMEM_LIMIT = 56 * 1024 * 1024


def _params(semantics):
    return pltpu.CompilerParams(dimension_semantics=semantics, vmem_limit_bytes=VMEM_LIMIT)


def _rms_norm_rows(x, g):
    return x * lax.rsqrt(jnp.mean(x * x, axis=-1, keepdims=True) + EPS) * g


def _dot(a, b):
    return jnp.dot(a, b, preferred_element_type=F32)


def _dot_nt(a, b):
    return lax.dot_general(a, b, (((1,), (1,)), ((), ())), preferred_element_type=F32)


def _norm_proj_kernel(x_ref, g_ref, w_ref, o_ref, xn_ref):
    @pl.when(pl.program_id(1) == 0)
    def _():
        xn_ref[...] = _rms_norm_rows(x_ref[...], g_ref[...]).astype(BF16)

    o_ref[...] = _dot(xn_ref[...], w_ref[...]).astype(o_ref.dtype)


def _norm_proj_gates_kernel(x_ref, g_ref, w_ref, wg_ref, o_ref, gates_ref, xn_ref):
    @pl.when(pl.program_id(1) == 0)
    def _():
        xn = _rms_norm_rows(x_ref[...], g_ref[...]).astype(BF16)
        xn_ref[...] = xn
        gates_ref[...] = _dot(xn, wg_ref[...])

    o_ref[...] = _dot(xn_ref[...], w_ref[...]).astype(o_ref.dtype)


def norm_proj(x, g, w, w_gates=None, *, n=None, tm=1024, tn=1024):
    m, d = x.shape
    n = w.shape[1] if n is None else n
    grid = (m // tm, n // tn)
    x_spec = pl.BlockSpec((tm, d), lambda i, j: (i, 0))
    g_spec = pl.BlockSpec((1, d), lambda i, j: (0, 0))
    w_spec = pl.BlockSpec((d, tn), lambda i, j: (0, j))
    o_spec = pl.BlockSpec((tm, tn), lambda i, j: (i, j))
    scratch = [pltpu.VMEM((tm, d), BF16)]
    if w_gates is None:
        return pl.pallas_call(
            _norm_proj_kernel,
            grid=grid,
            in_specs=[x_spec, g_spec, w_spec],
            out_specs=o_spec,
            out_shape=jax.ShapeDtypeStruct((m, n), BF16),
            scratch_shapes=scratch,
            compiler_params=_params(("parallel", "arbitrary")),
            name="norm_proj",
        )(x, g, w)
    ng = w_gates.shape[1]
    return pl.pallas_call(
        _norm_proj_gates_kernel,
        grid=grid,
        in_specs=[x_spec, g_spec, w_spec, pl.BlockSpec((d, ng), lambda i, j: (0, 0))],
        out_specs=[o_spec, pl.BlockSpec((tm, ng), lambda i, j: (i, 0))],
        out_shape=[jax.ShapeDtypeStruct((m, n), BF16), jax.ShapeDtypeStruct((m, ng), F32)],
        scratch_shapes=scratch,
        compiler_params=_params(("parallel", "arbitrary")),
        name="norm_proj_gates",
    )(x, g, w, w_gates)


def _proj_residual_kernel(a_ref, w_ref, r_ref, o_ref):
    o_ref[...] = r_ref[...] + _dot(a_ref[...], w_ref[...])


def proj_residual(a, w, r, *, tm=512, tn=2048):
    m, k = a.shape
    n = w.shape[1]
    return pl.pallas_call(
        _proj_residual_kernel,
        grid=(m // tm, n // tn),
        in_specs=[
            pl.BlockSpec((tm, k), lambda i, j: (i, 0)),
            pl.BlockSpec((k, tn), lambda i, j: (0, j)),
            pl.BlockSpec((tm, tn), lambda i, j: (i, j)),
        ],
        out_specs=pl.BlockSpec((tm, tn), lambda i, j: (i, j)),
        out_shape=jax.ShapeDtypeStruct((m, n), F32),
        compiler_params=_params(("parallel", "arbitrary")),
        name="proj_residual",
    )(a, w, r)


def _ffn_kernel(x_ref, g_ref, wg_ref, wu_ref, wd_ref, gf_ref, o_ref, xn_ref, *, final_norm):
    f = pl.program_id(1)

    @pl.when(f == 0)
    def _():
        x = x_ref[...]
        xn_ref[...] = _rms_norm_rows(x, g_ref[...]).astype(BF16)
        o_ref[...] = x

    xn = xn_ref[...]
    gate = _dot(xn, wg_ref[...])
    up = _dot(xn, wu_ref[...])
    act = (gate * jax.nn.sigmoid(gate) * up).astype(BF16)
    o_ref[...] += _dot(act, wd_ref[...])

    if final_norm:
        @pl.when(f == pl.num_programs(1) - 1)
        def _():
            o_ref[...] = _rms_norm_rows(o_ref[...], gf_ref[...])


def ffn_residual(x, g, w_gu, w_down, g_final, *, final_norm, tm=1024, tf=512):
    m, d = x.shape
    hidden = w_down.shape[0]
    nf = hidden // tf
    return pl.pallas_call(
        functools.partial(_ffn_kernel, final_norm=final_norm),
        grid=(m // tm, nf),
        in_specs=[
            pl.BlockSpec((tm, d), lambda i, f: (i, 0), pipeline_mode=pl.Buffered(1)),
            pl.BlockSpec((1, d), lambda i, f: (0, 0)),
            pl.BlockSpec((d, tf), lambda i, f: (0, f)),
            pl.BlockSpec((d, tf), lambda i, f: (0, f + nf)),
            pl.BlockSpec((tf, d), lambda i, f: (f, 0)),
            pl.BlockSpec((1, d), lambda i, f: (0, 0)),
        ],
        out_specs=pl.BlockSpec((tm, d), lambda i, f: (i, 0)),
        out_shape=jax.ShapeDtypeStruct((m, d), F32),
        scratch_shapes=[pltpu.VMEM((tm, d), BF16)],
        compiler_params=_params(("parallel", "arbitrary")),
        name="ffn_residual",
    )(x, g, w_gu, w_gu, w_down, g_final)


def _shift_rows(x, tail, s):
    xr = pltpu.roll(x, s, 0)
    tr = pltpu.roll(tail, s, 0)
    row = lax.broadcasted_iota(jnp.int32, tr.shape, 0)
    head = jnp.where(row < s, tr, xr[:SUBLANES])
    return jnp.concatenate([head, xr[SUBLANES:]], axis=0)


def _conv_silu(x, tail, w):
    acc = x * w[CONV_WIDTH - 1:CONV_WIDTH]
    for s in range(1, CONV_WIDTH):
        acc = acc + _shift_rows(x, tail, s) * w[CONV_WIDTH - 1 - s:CONV_WIDTH - s]
    return acc * jax.nn.sigmoid(acc)


def _log_sigmoid(x):
    return jnp.minimum(x, 0.0) - jnp.log1p(jnp.exp(-jnp.abs(x)))


def _mixer0_kernel(proj_ref, gates_ref, bg_ref, wconv_ref, gml_ref, gret_ref, cos_ref, sin_ref,
                   y_ref, c_ref, n_ref, m_ref, r_ref, tail_ref):
    L, HD, GW = CHUNK, HEAD_DIM0, GROUP_WIDTH

    @pl.when(pl.program_id(1) == 0)
    def _():
        c_ref[...] = jnp.zeros_like(c_ref)
        n_ref[...] = jnp.zeros_like(n_ref)
        m_ref[...] = jnp.zeros_like(m_ref)
        r_ref[...] = jnp.zeros_like(r_ref)
        tail_ref[...] = jnp.zeros_like(tail_ref)

    row = lax.broadcasted_iota(jnp.int32, (L, L), 0)
    col = lax.broadcasted_iota(jnp.int32, (L, L), 1)
    causal = col <= row
    diff = (row - col).astype(F32)

    gates = gates_ref[...] + bg_ref[...]
    gates_t = gates.T

    for h in range(ML_HEADS):
        qs = slice(h * HD, (h + 1) * HD)
        ks = slice(GW + h * HD, GW + (h + 1) * HD)
        q_raw = proj_ref[:, qs].astype(F32)
        k_raw = proj_ref[:, ks].astype(F32)
        q = _conv_silu(q_raw, tail_ref[:, qs], wconv_ref[:, qs]) * (HD ** -0.5)
        k = _conv_silu(k_raw, tail_ref[:, ks], wconv_ref[:, ks])
        tail_ref[:, qs] = q_raw[L - SUBLANES:]
        tail_ref[:, ks] = k_raw[L - SUBLANES:]
        v = proj_ref[:, 2 * GW + h * HD:2 * GW + (h + 1) * HD]
        o_gate = proj_ref[:, 3 * GW + h * HD:3 * GW + (h + 1) * HD].astype(F32)

        ig_col = gates[:, h:h + 1]
        lf_col = _log_sigmoid(gates[:, ML_HEADS + h:ML_HEADS + h + 1])
        ig_row = gates_t[h:h + 1, :]
        lf_row = _log_sigmoid(gates_t[ML_HEADS + h:ML_HEADS + h + 1, :])
        b_col = jnp.sum(jnp.where(causal, lf_row, 0.0), axis=1, keepdims=True)
        b_row = jnp.sum(jnp.where(row <= col, lf_col, 0.0), axis=0, keepdims=True)
        g_tot = jnp.sum(lf_row, axis=1, keepdims=True)
        a_col = g_tot - b_col + ig_col
        m_loc = jnp.max(a_col, axis=0, keepdims=True)
        w_end = jnp.exp(a_col - m_loc)

        m_prev = m_ref[h]
        log_d = jnp.where(causal, b_col - b_row + ig_row, NEG_BIG)
        m_inter = b_col + m_prev
        m_t = jnp.maximum(m_inter, jnp.max(log_d, axis=1, keepdims=True))
        q_b = q.astype(BF16)
        k_b = k.astype(BF16)
        w = _dot_nt(q_b, k_b) * jnp.exp(log_d - m_t)
        s_inter = jnp.exp(m_inter - m_t)
        c_prev = c_ref[h]
        n_prev = n_ref[h]
        num = _dot(w.astype(BF16), v) + s_inter * _dot(q_b, c_prev.astype(BF16))
        den = jnp.sum(w, axis=1, keepdims=True) + s_inter * jnp.sum(q * n_prev, axis=1, keepdims=True)
        hid = num / jnp.maximum(jnp.abs(den), jnp.exp(-m_t))

        m_new = jnp.maximum(g_tot + m_prev, m_loc)
        s_old = jnp.exp(g_tot + m_prev - m_new)
        s_new = jnp.exp(m_loc - m_new)
        kw = k * w_end
        c_ref[h] = s_old * c_prev + s_new * _dot(kw.T.astype(BF16), v)
        n_ref[h] = s_old * n_prev + s_new * jnp.sum(kw, axis=0, keepdims=True)
        m_ref[h] = m_new

        hid = hid * lax.rsqrt(jnp.mean(hid * hid, axis=1, keepdims=True) + EPS) * gml_ref[:, qs]
        y_ref[:, qs] = (hid * jax.nn.sigmoid(o_gate)).astype(y_ref.dtype)

    cos = cos_ref[...]
    sin = sin_ref[...]
    half = HD // 2

    def rotary(t):
        t1, t2 = t[:, :half], t[:, half:]
        return jnp.concatenate([t1 * cos - t2 * sin, t1 * sin + t2 * cos], axis=1)

    pos_col = lax.broadcasted_iota(jnp.int32, (L, 1), 0).astype(F32)
    for h in range(RET_HEADS):
        base = 4 * GW + h * HD
        log_gamma = float(np.log(np.float32(1.0) - np.float32(2.0) ** np.float32(-5.0 - 2.0 * h)))
        q = rotary(proj_ref[:, base:base + HD].astype(F32))
        k = rotary(proj_ref[:, base + GW:base + GW + HD].astype(F32)) * (HD ** -0.5)
        v = proj_ref[:, base + 2 * GW:base + 2 * GW + HD]
        r_gate = proj_ref[:, base + 3 * GW:base + 3 * GW + HD].astype(F32)

        decay_intra = jnp.where(causal, jnp.exp(jnp.maximum(diff, 0.0) * log_gamma), 0.0)
        decay_q = jnp.exp((pos_col + 1.0) * log_gamma)
        decay_k = jnp.exp((L - 1.0 - pos_col) * log_gamma)
        decay_chunk = math.exp(L * log_gamma)

        r_prev = r_ref[h]
        k_b = k.astype(BF16)
        scores = _dot_nt(q.astype(BF16), k_b) * decay_intra
        out = _dot(scores.astype(BF16), v) + _dot((q * decay_q).astype(BF16), r_prev.astype(BF16))
        r_ref[h] = decay_chunk * r_prev + _dot((k * decay_k).T.astype(BF16), v)

        out = out - jnp.mean(out, axis=1, keepdims=True)
        out = out * lax.rsqrt(jnp.mean(out * out, axis=1, keepdims=True) + EPS) * gret_ref[:, h * HD:(h + 1) * HD]
        y_ref[:, GW + h * HD:GW + (h + 1) * HD] = (out * (r_gate * jax.nn.sigmoid(r_gate))).astype(y_ref.dtype)


def mixer0(proj, gates, b_gates, w_conv, g_ml, g_ret, cos, sin):
    bsz, seq, width = proj.shape
    hd = HEAD_DIM0
    return pl.pallas_call(
        _mixer0_kernel,
        grid=(bsz, seq // CHUNK),
        in_specs=[
            pl.BlockSpec((None, CHUNK, width), lambda b, c: (b, c, 0)),
            pl.BlockSpec((None, CHUNK, LANES), lambda b, c: (b, c, 0)),
            pl.BlockSpec((1, LANES), lambda b, c: (0, 0)),
            pl.BlockSpec((CONV_WIDTH, 2 * GROUP_WIDTH), lambda b, c: (0, 0)),
            pl.BlockSpec((1, GROUP_WIDTH), lambda b, c: (0, 0)),
            pl.BlockSpec((1, GROUP_WIDTH), lambda b, c: (0, 0)),
            pl.BlockSpec((CHUNK, hd // 2), lambda b, c: (c, 0)),
            pl.BlockSpec((CHUNK, hd // 2), lambda b, c: (c, 0)),
        ],
        out_specs=pl.BlockSpec((None, CHUNK, 2 * GROUP_WIDTH), lambda b, c: (b, c, 0)),
        out_shape=jax.ShapeDtypeStruct((bsz, seq, 2 * GROUP_WIDTH), BF16),
        scratch_shapes=[
            pltpu.VMEM((ML_HEADS, hd, hd), F32),
            pltpu.VMEM((ML_HEADS, 1, hd), F32),
            pltpu.VMEM((ML_HEADS, 1, 1), F32),
            pltpu.VMEM((RET_HEADS, hd, hd), F32),
            pltpu.VMEM((SUBLANES, 2 * GROUP_WIDTH), F32),
        ],
        compiler_params=_params(("parallel", "arbitrary")),
        name="mixer0",
    )(proj, gates, b_gates, w_conv, g_ml, g_ret, cos, sin)


SB_BLOCK = 256
SB_GROUP = 4
SB_LOG_ZERO = -110.0
SB_BOUND_SLACK = 1.001


def _sb_kernel(q_ref, k_ref, v_ref, o_ref, kmax_ref):
    T, HD = SB_BLOCK, SB_HEAD_DIM
    qi = pl.program_id(2)
    heads = [slice(g * HD, (g + 1) * HD) for g in range(SB_GROUP)]

    @pl.when(qi == 0)
    def _():
        for g, hs in enumerate(heads):
            kf = k_ref[:, hs].astype(F32)
            ksq = jnp.max(jnp.sum(kf * kf, axis=1, keepdims=True), axis=0, keepdims=True)
            kmax_ref[g] = jnp.sqrt(ksq)

    row = lax.broadcasted_iota(jnp.int32, (T, T), 0)
    col = lax.broadcasted_iota(jnp.int32, (T, T), 1)
    ones_incl = jnp.where(row >= col, 1.0, 0.0).astype(BF16)
    cum_rhs = jnp.concatenate([ones_incl, ones_incl], axis=0)
    strict = col < row

    qs, zmax = [], []
    for g, hs in enumerate(heads):
        q = (q_ref[:, hs].astype(F32) * (HD ** -0.5)).astype(BF16)
        qf = q.astype(F32)
        qnorm = jnp.sqrt(jnp.sum(qf * qf, axis=1, keepdims=True))
        qs.append(q)
        zmax.append(qnorm * kmax_ref[g] * SB_BOUND_SLACK)

    def block(j, accs, rems, masked):
        start = pl.multiple_of(j * T, T)
        new_accs, new_rems = [], []
        for g, hs in enumerate(heads):
            k = k_ref[pl.ds(start, T), hs]
            v = v_ref[pl.ds(start, T), hs]
            z = _dot_nt(qs[g], k)
            log_keep = -(jnp.maximum(z, 0.0) + jnp.log(1.0 + jnp.exp(-jnp.abs(z))))
            if masked:
                log_keep = jnp.where(strict, log_keep, 0.0)
            hi = log_keep.astype(BF16)
            lo = (log_keep - hi.astype(F32)).astype(BF16)
            cum = _dot(jnp.concatenate([hi, lo], axis=1), cum_rhs)
            a = jnp.exp(z + cum + rems[g])
            if masked:
                a = jnp.where(strict, a, 0.0)
            new_accs.append(accs[g] + _dot(a.astype(BF16), v))
            new_rems.append(rems[g] + cum[:, 0:1])
        return tuple(new_accs), tuple(new_rems)

    def live(rems):
        worst = rems[0] + zmax[0]
        for g in range(1, SB_GROUP):
            worst = jnp.maximum(worst, rems[g] + zmax[g])
        return jnp.max(worst) >= SB_LOG_ZERO

    accs = tuple(jnp.zeros((T, HD), F32) for _ in heads)
    rems = tuple(jnp.zeros((T, 1), F32) for _ in heads)
    accs, rems = block(qi, accs, rems, True)

    def cond(carry):
        return jnp.logical_and(carry[0] >= 0, carry[1])

    def body(carry):
        j, _, accs, rems = carry
        accs, rems = block(j, accs, rems, False)
        return j - 1, live(rems), accs, rems

    _, _, accs, _ = lax.while_loop(cond, body, (qi - 1, live(rems), accs, rems))
    for g, hs in enumerate(heads):
        o_ref[:, hs] = accs[g].astype(o_ref.dtype)


def stick_breaking(qkv):
    bsz, seq, _ = qkv.shape
    T = SB_BLOCK
    width = SB_GROUP * SB_HEAD_DIM
    groups = SB_HEADS // SB_GROUP
    return pl.pallas_call(
        _sb_kernel,
        grid=(bsz, groups, seq // T),
        in_specs=[
            pl.BlockSpec((None, T, width), lambda b, h, i: (b, i, h)),
            pl.BlockSpec((None, seq, width), lambda b, h, i: (b, 0, groups + h)),
            pl.BlockSpec((None, seq, width), lambda b, h, i: (b, 0, 2 * groups + h)),
        ],
        out_specs=pl.BlockSpec((None, T, width), lambda b, h, i: (b, i, h)),
        out_shape=jax.ShapeDtypeStruct((bsz, seq, D_MODEL), BF16),
        scratch_shapes=[pltpu.VMEM((SB_GROUP, 1, 1), F32)],
        compiler_params=_params(("parallel", "parallel", "arbitrary")),
        name="stick_breaking",
    )(qkv, qkv, qkv)


def _rotary_tables(seq):
    half = HEAD_DIM0 // 2
    inv = ROPE_BASE ** (-jnp.arange(0, HEAD_DIM0, 2, dtype=F32) / HEAD_DIM0)
    ang = jnp.arange(seq, dtype=F32)[:, None] * inv[None, :]
    assert ang.shape == (seq, half)
    return jnp.cos(ang), jnp.sin(ang)


def kernel(x, norm_mix0, w_in0, b_gates0, w_conv0, g_ml0, g_ret0, w_out0, norm_ffn0, w_gu0, w_down0,
           norm_mix1, w_qkv1, w_out1, norm_ffn1, w_gu1, w_down1, final_norm):
    bsz, seq, d = x.shape
    m = bsz * seq
    row = lambda t: t.reshape(1, -1).astype(F32)
    n_main = 8 * GROUP_WIDTH
    n_gates = 2 * ML_HEADS

    xf = x.reshape(m, d)

    w_in = w_in0.astype(BF16)
    w_gates = jnp.pad(w_in[:, n_main:], ((0, 0), (0, LANES - n_gates)))
    b_gates = jnp.pad(b_gates0.astype(F32), (0, LANES - n_gates)).reshape(1, LANES)
    proj, gates = norm_proj(xf, row(norm_mix0), w_in, w_gates, n=n_main)
    cos, sin = _rotary_tables(seq)
    y = mixer0(proj.reshape(bsz, seq, n_main), gates.reshape(bsz, seq, LANES), b_gates,
               w_conv0.astype(F32), row(g_ml0), row(g_ret0), cos, sin)
    xf = proj_residual(y.reshape(m, d), w_out0.astype(BF16), xf)
    xf = ffn_residual(xf, row(norm_ffn0), w_gu0.astype(BF16), w_down0.astype(BF16), row(final_norm),
                      final_norm=False)

    qkv = norm_proj(xf, row(norm_mix1), w_qkv1.astype(BF16))
    o = stick_breaking(qkv.reshape(bsz, seq, 3 * d))
    xf = proj_residual(o.reshape(m, d), w_out1.astype(BF16), xf)
    xf = ffn_residual(xf, row(norm_ffn1), w_gu1.astype(BF16), w_down1.astype(BF16), row(final_norm),
                      final_norm=True)
    return xf.reshape(bsz, seq, d)
```

```python
import functools
import math

import numpy as np
import jax
import jax.numpy as jnp
from jax import lax
from jax.experimental import pallas as pl
from jax.experimental.pallas import tpu as pltpu

F32 = jnp.float32
BF16 = jnp.bfloat16

D_MODEL = 2048
GROUP_WIDTH = 1024
ML_HEADS = 4
RET_HEADS = 4
HEAD_DIM0 = 256
SB_HEADS = 16
SB_HEAD_DIM = 128
CHUNK = 128
CONV_WIDTH = 4
FFN_HIDDEN = 5632
ROPE_BASE = 10000.0
EPS = 1e-6
LANES = 128
TAIL_ROWS = 16
NEG_BIG = -1e30

VMEM_LIMIT = 56 * 1024 * 1024


def _params(semantics):
    return pltpu.CompilerParams(dimension_semantics=semantics, vmem_limit_bytes=VMEM_LIMIT)


def _rms_norm_rows(x, g):
    return x * lax.rsqrt(jnp.mean(x * x, axis=-1, keepdims=True) + EPS) * g


def _dot(a, b):
    return jnp.dot(a, b, preferred_element_type=F32)


def _dot_nt(a, b):
    return lax.dot_general(a, b, (((1,), (1,)), ((), ())), preferred_element_type=F32)


def _norm_proj_kernel(x_ref, g_ref, w_ref, o_ref, xn_ref):
    @pl.when(pl.program_id(1) == 0)
    def _():
        xn_ref[...] = _rms_norm_rows(x_ref[...], g_ref[...]).astype(BF16)

    o_ref[...] = _dot(xn_ref[...], w_ref[...]).astype(o_ref.dtype)


def _norm_proj_gates_kernel(x_ref, g_ref, w_ref, wg_ref, o_ref, gates_ref, xn_ref):
    @pl.when(pl.program_id(1) == 0)
    def _():
        xn = _rms_norm_rows(x_ref[...], g_ref[...]).astype(BF16)
        xn_ref[...] = xn
        gates_ref[...] = _dot(xn, wg_ref[...])

    o_ref[...] = _dot(xn_ref[...], w_ref[...]).astype(o_ref.dtype)


def norm_proj(x, g, w, w_gates=None, *, n=None, tm=1024, tn=1024):
    m, d = x.shape
    n = w.shape[1] if n is None else n
    grid = (m // tm, n // tn)
    x_spec = pl.BlockSpec((tm, d), lambda i, j: (i, 0))
    g_spec = pl.BlockSpec((1, d), lambda i, j: (0, 0))
    w_spec = pl.BlockSpec((d, tn), lambda i, j: (0, j))
    o_spec = pl.BlockSpec((tm, tn), lambda i, j: (i, j))
    scratch = [pltpu.VMEM((tm, d), BF16)]
    if w_gates is None:
        return pl.pallas_call(
            _norm_proj_kernel,
            grid=grid,
            in_specs=[x_spec, g_spec, w_spec],
            out_specs=o_spec,
            out_shape=jax.ShapeDtypeStruct((m, n), BF16),
            scratch_shapes=scratch,
            compiler_params=_params(("parallel", "arbitrary")),
            name="norm_proj",
        )(x, g, w)
    ng = w_gates.shape[1]
    return pl.pallas_call(
        _norm_proj_gates_kernel,
        grid=grid,
        in_specs=[x_spec, g_spec, w_spec, pl.BlockSpec((d, ng), lambda i, j: (0, 0))],
        out_specs=[o_spec, pl.BlockSpec((tm, ng), lambda i, j: (i, 0))],
        out_shape=[jax.ShapeDtypeStruct((m, n), BF16), jax.ShapeDtypeStruct((m, ng), F32)],
        scratch_shapes=scratch,
        compiler_params=_params(("parallel", "arbitrary")),
        name="norm_proj_gates",
    )(x, g, w, w_gates)


def _proj_residual_kernel(a_ref, w_ref, r_ref, o_ref):
    o_ref[...] = r_ref[...] + _dot(a_ref[...], w_ref[...])


def proj_residual(a, w, r, *, tm=512, tn=2048):
    m, k = a.shape
    n = w.shape[1]
    return pl.pallas_call(
        _proj_residual_kernel,
        grid=(m // tm, n // tn),
        in_specs=[
            pl.BlockSpec((tm, k), lambda i, j: (i, 0)),
            pl.BlockSpec((k, tn), lambda i, j: (0, j)),
            pl.BlockSpec((tm, tn), lambda i, j: (i, j)),
        ],
        out_specs=pl.BlockSpec((tm, tn), lambda i, j: (i, j)),
        out_shape=jax.ShapeDtypeStruct((m, n), F32),
        compiler_params=_params(("parallel", "arbitrary")),
        name="proj_residual",
    )(a, w, r)


def _ffn_kernel(x_ref, g_ref, wg_ref, wu_ref, wd_ref, gf_ref, o_ref, xn_ref, *, final_norm):
    f = pl.program_id(1)

    @pl.when(f == 0)
    def _():
        x = x_ref[...]
        xn_ref[...] = _rms_norm_rows(x, g_ref[...]).astype(BF16)
        o_ref[...] = x

    xn = xn_ref[...]
    gate = _dot(xn, wg_ref[...])
    up = _dot(xn, wu_ref[...])
    act = (gate * jax.nn.sigmoid(gate) * up).astype(BF16)
    o_ref[...] += _dot(act, wd_ref[...])

    if final_norm:
        @pl.when(f == pl.num_programs(1) - 1)
        def _():
            o_ref[...] = _rms_norm_rows(o_ref[...], gf_ref[...])


def ffn_residual(x, g, w_gu, w_down, g_final, *, final_norm, tm=1024, tf=512):
    m, d = x.shape
    hidden = w_down.shape[0]
    nf = hidden // tf
    return pl.pallas_call(
        functools.partial(_ffn_kernel, final_norm=final_norm),
        grid=(m // tm, nf),
        in_specs=[
            pl.BlockSpec((tm, d), lambda i, f: (i, 0)),
            pl.BlockSpec((1, d), lambda i, f: (0, 0)),
            pl.BlockSpec((d, tf), lambda i, f: (0, f)),
            pl.BlockSpec((d, tf), lambda i, f: (0, f + nf)),
            pl.BlockSpec((tf, d), lambda i, f: (f, 0)),
            pl.BlockSpec((1, d), lambda i, f: (0, 0)),
        ],
        out_specs=pl.BlockSpec((tm, d), lambda i, f: (i, 0)),
        out_shape=jax.ShapeDtypeStruct((m, d), F32),
        scratch_shapes=[pltpu.VMEM((tm, d), BF16)],
        compiler_params=_params(("parallel", "arbitrary")),
        name="ffn_residual",
    )(x, g, w_gu, w_gu, w_down, g_final)


def _shift_matrix(length):
    rows = (CONV_WIDTH - 1) * length
    r = lax.broadcasted_iota(jnp.int32, (rows, 2 * length), 0)
    c = lax.broadcasted_iota(jnp.int32, (rows, 2 * length), 1)
    sel = jnp.zeros((rows, 2 * length), jnp.bool_)
    for s in range(1, CONV_WIDTH):
        in_band = jnp.logical_and(r >= (s - 1) * length, r < s * length)
        sel = jnp.logical_or(sel, jnp.logical_and(in_band, c == r - (s - 1) * length + length - s))
    return jnp.where(sel, 1.0, 0.0).astype(BF16)


def _conv_silu(x_b, tail_b, w, shift):
    length, ch = x_b.shape
    pad = jnp.zeros((length - tail_b.shape[0], ch), BF16)
    shifted = _dot(shift, jnp.concatenate([pad, tail_b, x_b], axis=0))
    acc = x_b.astype(F32) * w[CONV_WIDTH - 1:CONV_WIDTH]
    for s in range(1, CONV_WIDTH):
        acc = acc + shifted[(s - 1) * length:s * length] * w[CONV_WIDTH - 1 - s:CONV_WIDTH - s]
    return acc * jax.nn.sigmoid(acc)


def _log_sigmoid(x):
    return jnp.minimum(x, 0.0) - jnp.log1p(jnp.exp(-jnp.abs(x)))


def _mixer0_kernel(proj_ref, gates_ref, bg_ref, wconv_ref, gml_ref, gret_ref, cos_ref, sin_ref,
                   y_ref, c_ref, n_ref, m_ref, r_ref, tail_ref):
    L, HD, GW = CHUNK, HEAD_DIM0, GROUP_WIDTH

    @pl.when(pl.program_id(1) == 0)
    def _():
        c_ref[...] = jnp.zeros_like(c_ref)
        n_ref[...] = jnp.zeros_like(n_ref)
        m_ref[...] = jnp.zeros_like(m_ref)
        r_ref[...] = jnp.zeros_like(r_ref)
        tail_ref[...] = jnp.zeros_like(tail_ref)

    row = lax.broadcasted_iota(jnp.int32, (L, L), 0)
    col = lax.broadcasted_iota(jnp.int32, (L, L), 1)
    causal = col <= row
    diff = (row - col).astype(F32)

    gates = gates_ref[...] + bg_ref[...]
    gates_t = gates.T

    cos = cos_ref[...]
    sin = sin_ref[...]
    half = HD // 2
    pos_col = lax.broadcasted_iota(jnp.int32, (L, 1), 0).astype(F32)
    shift = _shift_matrix(L)

    def rotary(t):
        t1, t2 = t[:, :half], t[:, half:]
        return jnp.concatenate([t1 * cos - t2 * sin, t1 * sin + t2 * cos], axis=1)


    def ml_operands(h):
        qs = slice(h * HD, (h + 1) * HD)
        ks = slice(GW + h * HD, GW + (h + 1) * HD)
        q_raw = proj_ref[:, qs]
        k_raw = proj_ref[:, ks]
        q = _conv_silu(q_raw, tail_ref[:, qs], wconv_ref[:, qs], shift) * (HD ** -0.5)
        k = _conv_silu(k_raw, tail_ref[:, ks], wconv_ref[:, ks], shift)
        tail_ref[:, qs] = q_raw[L - TAIL_ROWS:]
        tail_ref[:, ks] = k_raw[L - TAIL_ROWS:]

        ig_col = gates[:, h:h + 1]
        lf_col = _log_sigmoid(gates[:, ML_HEADS + h:ML_HEADS + h + 1])
        ig_row = gates_t[h:h + 1, :]
        lf_row = _log_sigmoid(gates_t[ML_HEADS + h:ML_HEADS + h + 1, :])
        b_col = jnp.sum(jnp.where(causal, lf_row, 0.0), axis=1, keepdims=True)
        b_row = jnp.sum(jnp.where(row <= col, lf_col, 0.0), axis=0, keepdims=True)
        g_tot = jnp.sum(lf_row, axis=1, keepdims=True)
        a_col = g_tot - b_col + ig_col
        m_loc = jnp.max(a_col, axis=0, keepdims=True)
        w_end = jnp.exp(a_col - m_loc)

        m_prev = m_ref[h]
        log_d = jnp.where(causal, b_col - b_row + ig_row, NEG_BIG)
        m_inter = b_col + m_prev
        m_t = jnp.maximum(m_inter, jnp.max(log_d, axis=1, keepdims=True))
        m_new = jnp.maximum(g_tot + m_prev, m_loc)
        kw = k * w_end
        return dict(
            h=h, q=q, q_b=q.astype(BF16), k_b=k.astype(BF16), kw_t=kw.T.astype(BF16),
            kw_sum=jnp.sum(kw, axis=0, keepdims=True), decay=jnp.exp(log_d - m_t),
            s_inter=jnp.exp(m_inter - m_t), floor=jnp.exp(-m_t), m_new=m_new,
            s_old=jnp.exp(g_tot + m_prev - m_new), s_new=jnp.exp(m_loc - m_new))

    def ml_first(p):
        h = p["h"]
        v = proj_ref[:, 2 * GW + h * HD:2 * GW + (h + 1) * HD]
        p["qk"] = _dot_nt(p["q_b"], p["k_b"])
        p["qc"] = _dot(p["q_b"], c_ref[h].astype(BF16))
        p["kv"] = _dot(p["kw_t"], v)

    def ml_second(p):
        h = p["h"]
        qs = slice(h * HD, (h + 1) * HD)
        v = proj_ref[:, 2 * GW + h * HD:2 * GW + (h + 1) * HD]
        o_gate = proj_ref[:, 3 * GW + h * HD:3 * GW + (h + 1) * HD].astype(F32)
        n_prev = n_ref[h]
        w = p["qk"] * p["decay"]
        num = _dot(w.astype(BF16), v) + p["s_inter"] * p["qc"]
        den = jnp.sum(w, axis=1, keepdims=True) + p["s_inter"] * jnp.sum(p["q"] * n_prev, axis=1, keepdims=True)
        hid = num / jnp.maximum(jnp.abs(den), p["floor"])
        c_ref[h] = p["s_old"] * c_ref[h] + p["s_new"] * p["kv"]
        n_ref[h] = p["s_old"] * n_prev + p["s_new"] * p["kw_sum"]
        m_ref[h] = p["m_new"]
        hid = hid * lax.rsqrt(jnp.mean(hid * hid, axis=1, keepdims=True) + EPS) * gml_ref[:, qs]
        y_ref[:, qs] = (hid * jax.nn.sigmoid(o_gate)).astype(y_ref.dtype)

    def ret_operands(h):
        base = 4 * GW + h * HD
        log_gamma = float(np.log(np.float32(1.0) - np.float32(2.0) ** np.float32(-5.0 - 2.0 * h)))
        q = rotary(proj_ref[:, base:base + HD].astype(F32))
        k = rotary(proj_ref[:, base + GW:base + GW + HD].astype(F32)) * (HD ** -0.5)
        decay_q = jnp.exp((pos_col + 1.0) * log_gamma)
        decay_k = jnp.exp((L - 1.0 - pos_col) * log_gamma)
        return dict(
            h=h, q_b=q.astype(BF16), k_b=k.astype(BF16), qd_b=(q * decay_q).astype(BF16),
            kd_t=(k * decay_k).T.astype(BF16),
            decay=jnp.where(causal, jnp.exp(jnp.maximum(diff, 0.0) * log_gamma), 0.0),
            decay_chunk=math.exp(L * log_gamma))

    def ret_first(p):
        h = p["h"]
        v = proj_ref[:, 6 * GW + h * HD:6 * GW + (h + 1) * HD]
        p["qk"] = _dot_nt(p["q_b"], p["k_b"])
        p["qr"] = _dot(p["qd_b"], r_ref[h].astype(BF16))
        p["kv"] = _dot(p["kd_t"], v)

    def ret_second(p):
        h = p["h"]
        v = proj_ref[:, 6 * GW + h * HD:6 * GW + (h + 1) * HD]
        r_gate = proj_ref[:, 7 * GW + h * HD:7 * GW + (h + 1) * HD].astype(F32)
        out = _dot((p["qk"] * p["decay"]).astype(BF16), v) + p["qr"]
        r_ref[h] = p["decay_chunk"] * r_ref[h] + p["kv"]
        out = out - jnp.mean(out, axis=1, keepdims=True)
        out = out * lax.rsqrt(jnp.mean(out * out, axis=1, keepdims=True) + EPS) * gret_ref[:, h * HD:(h + 1) * HD]
        y_ref[:, GW + h * HD:GW + (h + 1) * HD] = (out * (r_gate * jax.nn.sigmoid(r_gate))).astype(y_ref.dtype)

    ml = [ml_operands(h) for h in range(ML_HEADS)]
    ret = [ret_operands(h) for h in range(RET_HEADS)]
    for p in ml:
        ml_first(p)
    for p in ret:
        ret_first(p)
    for p in ml:
        ml_second(p)
    for p in ret:
        ret_second(p)


def mixer0(proj, gates, b_gates, w_conv, g_ml, g_ret, cos, sin):
    bsz, seq, width = proj.shape
    hd = HEAD_DIM0
    return pl.pallas_call(
        _mixer0_kernel,
        grid=(bsz, seq // CHUNK),
        in_specs=[
            pl.BlockSpec((None, CHUNK, width), lambda b, c: (b, c, 0)),
            pl.BlockSpec((None, CHUNK, LANES), lambda b, c: (b, c, 0)),
            pl.BlockSpec((1, LANES), lambda b, c: (0, 0)),
            pl.BlockSpec((CONV_WIDTH, 2 * GROUP_WIDTH), lambda b, c: (0, 0)),
            pl.BlockSpec((1, GROUP_WIDTH), lambda b, c: (0, 0)),
            pl.BlockSpec((1, GROUP_WIDTH), lambda b, c: (0, 0)),
            pl.BlockSpec((CHUNK, hd // 2), lambda b, c: (c, 0)),
            pl.BlockSpec((CHUNK, hd // 2), lambda b, c: (c, 0)),
        ],
        out_specs=pl.BlockSpec((None, CHUNK, 2 * GROUP_WIDTH), lambda b, c: (b, c, 0)),
        out_shape=jax.ShapeDtypeStruct((bsz, seq, 2 * GROUP_WIDTH), BF16),
        scratch_shapes=[
            pltpu.VMEM((ML_HEADS, hd, hd), F32),
            pltpu.VMEM((ML_HEADS, 1, hd), F32),
            pltpu.VMEM((ML_HEADS, 1, 1), F32),
            pltpu.VMEM((RET_HEADS, hd, hd), F32),
            pltpu.VMEM((TAIL_ROWS, 2 * GROUP_WIDTH), BF16),
        ],
        compiler_params=_params(("parallel", "arbitrary")),
        name="mixer0",
    )(proj, gates, b_gates, w_conv, g_ml, g_ret, cos, sin)


SB_BLOCK = 256
SB_GROUP = 4
SB_LOG_ZERO = -110.0
SB_BOUND_SLACK = 1.001


def _sb_kernel(q_ref, k_ref, v_ref, o_ref, kmax_ref):
    T, HD = SB_BLOCK, SB_HEAD_DIM
    qi = pl.program_id(2)
    heads = [slice(g * HD, (g + 1) * HD) for g in range(SB_GROUP)]

    @pl.when(qi == 0)
    def _():
        for g, hs in enumerate(heads):
            kf = k_ref[:, hs].astype(F32)
            ksq = jnp.max(jnp.sum(kf * kf, axis=1, keepdims=True), axis=0, keepdims=True)
            kmax_ref[g] = jnp.sqrt(ksq)

    row = lax.broadcasted_iota(jnp.int32, (T, T), 0)
    col = lax.broadcasted_iota(jnp.int32, (T, T), 1)
    ones_incl = jnp.where(row >= col, 1.0, 0.0).astype(BF16)
    cum_rhs = jnp.concatenate([ones_incl, ones_incl], axis=0)
    strict = col < row

    qs, zmax = [], []
    for g, hs in enumerate(heads):
        q = (q_ref[:, hs].astype(F32) * (HD ** -0.5)).astype(BF16)
        qf = q.astype(F32)
        qnorm = jnp.sqrt(jnp.sum(qf * qf, axis=1, keepdims=True))
        qs.append(q)
        zmax.append(qnorm * kmax_ref[g] * SB_BOUND_SLACK)

    def span(start, nsub, accs, rems, keep):
        zs = [_dot_nt(qs[g], k_ref[pl.ds(start, nsub * T), hs]) for g, hs in enumerate(heads)]
        cums = []
        for g in range(SB_GROUP):
            z = zs[g]
            log_keep = -(jnp.maximum(z, 0.0) + jnp.log(1.0 + jnp.exp(-jnp.abs(z))))
            split = []
            for u in range(nsub):
                part = log_keep[:, u * T:(u + 1) * T]
                if keep[u] is not None:
                    part = jnp.where(keep[u], part, 0.0)
                hi = part.astype(BF16)
                lo = (part - hi.astype(F32)).astype(BF16)
                split.append(jnp.concatenate([hi, lo], axis=1))
            cums.append(_dot(jnp.concatenate(split, axis=0), cum_rhs))
        new_accs, new_rems = [], []
        for g, hs in enumerate(heads):
            rem = rems[g]
            weights = [None] * nsub
            for u in reversed(range(nsub)):
                cum_u = cums[g][u * T:(u + 1) * T]
                a = jnp.exp(zs[g][:, u * T:(u + 1) * T] + cum_u + rem)
                if keep[u] is not None:
                    a = jnp.where(keep[u], a, 0.0)
                weights[u] = a.astype(BF16)
                rem = rem + cum_u[:, 0:1]
            v = v_ref[pl.ds(start, nsub * T), hs]
            new_accs.append(accs[g] + _dot(jnp.concatenate(weights, axis=1), v))
            new_rems.append(rem)
        return tuple(new_accs), tuple(new_rems)

    def live(rems):
        worst = rems[0] + zmax[0]
        for g in range(1, SB_GROUP):
            worst = jnp.maximum(worst, rems[g] + zmax[g])
        return jnp.max(worst) >= SB_LOG_ZERO

    def zeros():
        return (tuple(jnp.zeros((T, HD), F32) for _ in heads), tuple(jnp.zeros((T, 1), F32) for _ in heads))

    def finish(accs):
        for g, hs in enumerate(heads):
            o_ref[:, hs] = accs[g].astype(o_ref.dtype)

    @pl.when(qi == 0)
    def _():
        accs, _ = span(0, 1, *zeros(), [strict])
        finish(accs)

    @pl.when(qi > 0)
    def _():
        accs, rems = span(pl.multiple_of((qi - 1) * T, T), 2, *zeros(), [None, strict])

        def cond(carry):
            return jnp.logical_and(carry[0] >= -1, carry[1])

        def body(carry):
            first, _, accs, rems = carry
            start = pl.multiple_of(jnp.maximum(first, 0) * T, T)
            accs, rems = span(start, 2, accs, rems, [None, first >= 0])
            return first - 2, live(rems), accs, rems

        _, _, accs, _ = lax.while_loop(cond, body, (qi - 3, live(rems), accs, rems))
        finish(accs)


def stick_breaking(qkv):
    bsz, seq, _ = qkv.shape
    T = SB_BLOCK
    width = SB_GROUP * SB_HEAD_DIM
    groups = SB_HEADS // SB_GROUP
    return pl.pallas_call(
        _sb_kernel,
        grid=(bsz, groups, seq // T),
        in_specs=[
            pl.BlockSpec((None, T, width), lambda b, h, i: (b, i, h)),
            pl.BlockSpec((None, seq, width), lambda b, h, i: (b, 0, groups + h)),
            pl.BlockSpec((None, seq, width), lambda b, h, i: (b, 0, 2 * groups + h)),
        ],
        out_specs=pl.BlockSpec((None, T, width), lambda b, h, i: (b, i, h)),
        out_shape=jax.ShapeDtypeStruct((bsz, seq, D_MODEL), BF16),
        scratch_shapes=[pltpu.VMEM((SB_GROUP, 1, 1), F32)],
        compiler_params=_params(("parallel", "parallel", "arbitrary")),
        name="stick_breaking",
    )(qkv, qkv, qkv)


def _rotary_tables(seq):
    half = HEAD_DIM0 // 2
    inv = ROPE_BASE ** (-jnp.arange(0, HEAD_DIM0, 2, dtype=F32) / HEAD_DIM0)
    ang = jnp.arange(seq, dtype=F32)[:, None] * inv[None, :]
    assert ang.shape == (seq, half)
    return jnp.cos(ang), jnp.sin(ang)


def kernel(x, norm_mix0, w_in0, b_gates0, w_conv0, g_ml0, g_ret0, w_out0, norm_ffn0, w_gu0, w_down0,
           norm_mix1, w_qkv1, w_out1, norm_ffn1, w_gu1, w_down1, final_norm):
    bsz, seq, d = x.shape
    m = bsz * seq
    row = lambda t: t.reshape(1, -1).astype(F32)
    n_main = 8 * GROUP_WIDTH
    n_gates = 2 * ML_HEADS

    xf = x.reshape(m, d)

    w_in = w_in0.astype(BF16)
    w_gates = jnp.pad(w_in[:, n_main:], ((0, 0), (0, LANES - n_gates)))
    b_gates = jnp.pad(b_gates0.astype(F32), (0, LANES - n_gates)).reshape(1, LANES)
    proj, gates = norm_proj(xf, row(norm_mix0), w_in, w_gates, n=n_main)
    cos, sin = _rotary_tables(seq)
    y = mixer0(proj.reshape(bsz, seq, n_main), gates.reshape(bsz, seq, LANES), b_gates,
               w_conv0.astype(F32), row(g_ml0), row(g_ret0), cos, sin)
    xf = proj_residual(y.reshape(m, d), w_out0.astype(BF16), xf)
    xf = ffn_residual(xf, row(norm_ffn0), w_gu0.astype(BF16), w_down0.astype(BF16), row(final_norm),
                      final_norm=False)

    qkv = norm_proj(xf, row(norm_mix1), w_qkv1.astype(BF16))
    o = stick_breaking(qkv.reshape(bsz, seq, 3 * d))
    xf = proj_residual(o.reshape(m, d), w_out1.astype(BF16), xf)
    xf = ffn_residual(xf, row(norm_ffn1), w_gu1.astype(BF16), w_down1.astype(BF16), row(final_norm),
                      final_norm=True)
    return xf.reshape(bsz, seq, d)
```

```python
import functools
import math

import numpy as np
import jax
import jax.numpy as jnp
from jax import lax
from jax.experimental import pallas as pl
from jax.experimental.pallas import tpu as pltpu

F32 = jnp.float32
BF16 = jnp.bfloat16

D_MODEL = 2048
GROUP_WIDTH = 1024
ML_HEADS = 4
RET_HEADS = 4
HEAD_DIM0 = 256
SB_HEADS = 16
SB_HEAD_DIM = 128
CHUNK = 128
CONV_WIDTH = 4
FFN_HIDDEN = 5632
ROPE_BASE = 10000.0
EPS = 1e-6
LANES = 128
TAIL_ROWS = 16
NEG_BIG = -1e30

VMEM_LIMIT = 60 * 1024 * 1024


def _params(semantics):
    return pltpu.CompilerParams(dimension_semantics=semantics, vmem_limit_bytes=VMEM_LIMIT)


def _rms_norm_rows(x, g):
    return x * lax.rsqrt(jnp.mean(x * x, axis=-1, keepdims=True) + EPS) * g


def _dot(a, b):
    return jnp.dot(a, b, preferred_element_type=F32)


def _dot_nt(a, b):
    return lax.dot_general(a, b, (((1,), (1,)), ((), ())), preferred_element_type=F32)


def _norm_proj_kernel(x_ref, g_ref, w_ref, o_ref, xn_ref):
    @pl.when(pl.program_id(1) == 0)
    def _():
        xn_ref[...] = _rms_norm_rows(x_ref[...], g_ref[...]).astype(BF16)

    o_ref[...] = _dot(xn_ref[...], w_ref[...]).astype(o_ref.dtype)


def _norm_proj_gates_kernel(x_ref, g_ref, w_ref, wg_ref, o_ref, gates_ref, xn_ref):
    @pl.when(pl.program_id(1) == 0)
    def _():
        xn = _rms_norm_rows(x_ref[...], g_ref[...]).astype(BF16)
        xn_ref[...] = xn
        gates_ref[...] = _dot(xn, wg_ref[...])

    o_ref[...] = _dot(xn_ref[...], w_ref[...]).astype(o_ref.dtype)


def norm_proj(x, g, w, w_gates=None, *, n=None, tm=1024, tn=2048):
    m, d = x.shape
    n = w.shape[1] if n is None else n
    grid = (m // tm, n // tn)
    x_spec = pl.BlockSpec((tm, d), lambda i, j: (i, 0))
    g_spec = pl.BlockSpec((1, d), lambda i, j: (0, 0))
    w_spec = pl.BlockSpec((d, tn), lambda i, j: (0, j))
    o_spec = pl.BlockSpec((tm, tn), lambda i, j: (i, j))
    scratch = [pltpu.VMEM((tm, d), BF16)]
    if w_gates is None:
        return pl.pallas_call(
            _norm_proj_kernel,
            grid=grid,
            in_specs=[x_spec, g_spec, w_spec],
            out_specs=o_spec,
            out_shape=jax.ShapeDtypeStruct((m, n), BF16),
            scratch_shapes=scratch,
            compiler_params=_params(("parallel", "arbitrary")),
            name="norm_proj",
        )(x, g, w)
    ng = w_gates.shape[1]
    return pl.pallas_call(
        _norm_proj_gates_kernel,
        grid=grid,
        in_specs=[x_spec, g_spec, w_spec, pl.BlockSpec((d, ng), lambda i, j: (0, 0))],
        out_specs=[o_spec, pl.BlockSpec((tm, ng), lambda i, j: (i, 0))],
        out_shape=[jax.ShapeDtypeStruct((m, n), BF16), jax.ShapeDtypeStruct((m, ng), F32)],
        scratch_shapes=scratch,
        compiler_params=_params(("parallel", "arbitrary")),
        name="norm_proj_gates",
    )(x, g, w, w_gates)


def _proj_residual_kernel(a_ref, w_ref, r_ref, o_ref):
    o_ref[...] = r_ref[...] + _dot(a_ref[...], w_ref[...])


def proj_residual(a, w, r, *, tm=512, tn=2048):
    m, k = a.shape
    n = w.shape[1]
    return pl.pallas_call(
        _proj_residual_kernel,
        grid=(m // tm, n // tn),
        in_specs=[
            pl.BlockSpec((tm, k), lambda i, j: (i, 0)),
            pl.BlockSpec((k, tn), lambda i, j: (0, j)),
            pl.BlockSpec((tm, tn), lambda i, j: (i, j)),
        ],
        out_specs=pl.BlockSpec((tm, tn), lambda i, j: (i, j)),
        out_shape=jax.ShapeDtypeStruct((m, n), F32),
        compiler_params=_params(("parallel", "arbitrary")),
        name="proj_residual",
    )(a, w, r)


def _ffn_step(f, last, x_ref, g_ref, wg, wu, wd, gf_ref, o_ref, xn_ref, final_norm):
    @pl.when(f == 0)
    def _():
        x = x_ref[...]
        xn_ref[...] = _rms_norm_rows(x, g_ref[...]).astype(BF16)
        o_ref[...] = x

    xn = xn_ref[...]
    gate = _dot(xn, wg)
    up = _dot(xn, wu)
    act = (gate * jax.nn.sigmoid(gate) * up).astype(BF16)
    o_ref[...] += _dot(act, wd)

    if final_norm:
        @pl.when(f == last)
        def _():
            o_ref[...] = _rms_norm_rows(o_ref[...], gf_ref[...])


def _ffn_head_kernel(x_ref, g_ref, wg_ref, wu_ref, wd_ref, gf_ref, o_ref, wg16_ref, wu16_ref, wd16_ref, xn_ref,
                     *, final_norm):
    wg = wg_ref[...].astype(BF16)
    wu = wu_ref[...].astype(BF16)
    wd = wd_ref[...].astype(BF16)
    wg16_ref[...] = wg
    wu16_ref[...] = wu
    wd16_ref[...] = wd
    _ffn_step(pl.program_id(0), pl.num_programs(0) - 1, x_ref, g_ref, wg, wu, wd, gf_ref, o_ref, xn_ref, final_norm)


def _ffn_kernel(x_ref, g_ref, wg_ref, wu_ref, wd_ref, gf_ref, head_hbm_ref, o_ref, xn_ref, head_sem,
                *, final_norm):
    i = pl.program_id(0)
    f = pl.program_id(1)

    @pl.when(jnp.logical_and(i == 0, f == 0))
    def _():
        fetch = pltpu.make_async_copy(head_hbm_ref, o_ref, head_sem)
        fetch.start()
        fetch.wait()

    @pl.when(i > 0)
    def _():
        _ffn_step(f, pl.num_programs(1) - 1, x_ref, g_ref, wg_ref[...], wu_ref[...], wd_ref[...], gf_ref, o_ref,
                  xn_ref, final_norm)


def ffn_residual(x, g, w_gu, w_down, g_final, *, final_norm, tm=1024, tf=512, tf_head=256):
    m, d = x.shape
    hidden = w_down.shape[0]
    nh = hidden // tf_head
    row_spec = pl.BlockSpec((1, d), lambda *_: (0, 0))
    head, wg16, wu16, wd16 = pl.pallas_call(
        functools.partial(_ffn_head_kernel, final_norm=final_norm),
        grid=(nh,),
        in_specs=[
            pl.BlockSpec((tm, d), lambda f: (0, 0), pipeline_mode=pl.Buffered(1)),
            row_spec,
            pl.BlockSpec((d, tf_head), lambda f: (0, f)),
            pl.BlockSpec((d, tf_head), lambda f: (0, f + nh)),
            pl.BlockSpec((tf_head, d), lambda f: (f, 0)),
            row_spec,
        ],
        out_specs=[
            pl.BlockSpec((tm, d), lambda f: (0, 0)),
            pl.BlockSpec((d, tf_head), lambda f: (0, f)),
            pl.BlockSpec((d, tf_head), lambda f: (0, f)),
            pl.BlockSpec((tf_head, d), lambda f: (f, 0)),
        ],
        out_shape=[
            jax.ShapeDtypeStruct((tm, d), F32),
            jax.ShapeDtypeStruct((d, hidden), BF16),
            jax.ShapeDtypeStruct((d, hidden), BF16),
            jax.ShapeDtypeStruct((hidden, d), BF16),
        ],
        scratch_shapes=[pltpu.VMEM((tm, d), BF16)],
        compiler_params=_params(("arbitrary",)),
        name="ffn_head",
    )(x, g, w_gu, w_gu, w_down, g_final)

    nf = hidden // tf
    held = lambda i, f: jnp.where(i == 0, 0, f)
    return pl.pallas_call(
        functools.partial(_ffn_kernel, final_norm=final_norm),
        grid=(m // tm, nf),
        in_specs=[
            pl.BlockSpec((tm, d), lambda i, f: (i, 0)),
            row_spec,
            pl.BlockSpec((d, tf), lambda i, f: (0, held(i, f))),
            pl.BlockSpec((d, tf), lambda i, f: (0, held(i, f))),
            pl.BlockSpec((tf, d), lambda i, f: (held(i, f), 0)),
            row_spec,
            pl.BlockSpec(memory_space=pl.ANY),
        ],
        out_specs=pl.BlockSpec((tm, d), lambda i, f: (i, 0)),
        out_shape=jax.ShapeDtypeStruct((m, d), F32),
        scratch_shapes=[pltpu.VMEM((tm, d), BF16), pltpu.SemaphoreType.DMA(())],
        compiler_params=_params(("arbitrary", "arbitrary")),
        name="ffn_residual",
    )(x, g, wg16, wu16, wd16, g_final, head)


def _shift_matrix(length):
    rows = (CONV_WIDTH - 1) * length
    r = lax.broadcasted_iota(jnp.int32, (rows, 2 * length), 0)
    c = lax.broadcasted_iota(jnp.int32, (rows, 2 * length), 1)
    sel = jnp.zeros((rows, 2 * length), jnp.bool_)
    for s in range(1, CONV_WIDTH):
        in_band = jnp.logical_and(r >= (s - 1) * length, r < s * length)
        sel = jnp.logical_or(sel, jnp.logical_and(in_band, c == r - (s - 1) * length + length - s))
    return jnp.where(sel, 1.0, 0.0).astype(BF16)


def _conv_silu(x_b, tail_b, w, shift):
    length, ch = x_b.shape
    pad = jnp.zeros((length - tail_b.shape[0], ch), BF16)
    shifted = _dot(shift, jnp.concatenate([pad, tail_b, x_b], axis=0))
    acc = x_b.astype(F32) * w[CONV_WIDTH - 1:CONV_WIDTH]
    for s in range(1, CONV_WIDTH):
        acc = acc + shifted[(s - 1) * length:s * length] * w[CONV_WIDTH - 1 - s:CONV_WIDTH - s]
    return acc * jax.nn.sigmoid(acc)


def _log_sigmoid(x):
    return jnp.minimum(x, 0.0) - jnp.log1p(jnp.exp(-jnp.abs(x)))


def _mixer0_kernel(proj_ref, gates_ref, bg_ref, wconv_ref, gml_ref, gret_ref, cos_ref, sin_ref,
                   y_ref, c_ref, n_ref, m_ref, r_ref, tail_ref):
    L, HD, GW = CHUNK, HEAD_DIM0, GROUP_WIDTH

    @pl.when(pl.program_id(1) == 0)
    def _():
        c_ref[...] = jnp.zeros_like(c_ref)
        n_ref[...] = jnp.zeros_like(n_ref)
        m_ref[...] = jnp.zeros_like(m_ref)
        r_ref[...] = jnp.zeros_like(r_ref)
        tail_ref[...] = jnp.zeros_like(tail_ref)

    row = lax.broadcasted_iota(jnp.int32, (L, L), 0)
    col = lax.broadcasted_iota(jnp.int32, (L, L), 1)
    causal = col <= row
    diff = (row - col).astype(F32)

    gates = gates_ref[...] + bg_ref[...]
    gates_t = gates.T

    cos = cos_ref[...]
    sin = sin_ref[...]
    half = HD // 2
    pos_col = lax.broadcasted_iota(jnp.int32, (L, 1), 0).astype(F32)
    shift = _shift_matrix(L)

    def rotary(t):
        t1, t2 = t[:, :half], t[:, half:]
        return jnp.concatenate([t1 * cos - t2 * sin, t1 * sin + t2 * cos], axis=1)


    def ml_operands(h):
        qs = slice(h * HD, (h + 1) * HD)
        ks = slice(GW + h * HD, GW + (h + 1) * HD)
        q_raw = proj_ref[:, qs]
        k_raw = proj_ref[:, ks]
        q = _conv_silu(q_raw, tail_ref[:, qs], wconv_ref[:, qs], shift) * (HD ** -0.5)
        k = _conv_silu(k_raw, tail_ref[:, ks], wconv_ref[:, ks], shift)
        tail_ref[:, qs] = q_raw[L - TAIL_ROWS:]
        tail_ref[:, ks] = k_raw[L - TAIL_ROWS:]

        ig_col = gates[:, h:h + 1]
        lf_col = _log_sigmoid(gates[:, ML_HEADS + h:ML_HEADS + h + 1])
        ig_row = gates_t[h:h + 1, :]
        lf_row = _log_sigmoid(gates_t[ML_HEADS + h:ML_HEADS + h + 1, :])
        b_col = jnp.sum(jnp.where(causal, lf_row, 0.0), axis=1, keepdims=True)
        b_row = jnp.sum(jnp.where(row <= col, lf_col, 0.0), axis=0, keepdims=True)
        g_tot = jnp.sum(lf_row, axis=1, keepdims=True)
        a_col = g_tot - b_col + ig_col
        m_loc = jnp.max(a_col, axis=0, keepdims=True)
        w_end = jnp.exp(a_col - m_loc)

        m_prev = m_ref[h]
        log_d = jnp.where(causal, b_col - b_row + ig_row, NEG_BIG)
        m_inter = b_col + m_prev
        m_t = jnp.maximum(m_inter, jnp.max(log_d, axis=1, keepdims=True))
        m_new = jnp.maximum(g_tot + m_prev, m_loc)
        kw = k * w_end
        return dict(
            h=h, q=q, q_b=q.astype(BF16), k_b=k.astype(BF16), kw_t=kw.T.astype(BF16),
            kw_sum=jnp.sum(kw, axis=0, keepdims=True), decay=jnp.exp(log_d - m_t),
            s_inter=jnp.exp(m_inter - m_t), floor=jnp.exp(-m_t), m_new=m_new,
            s_old=jnp.exp(g_tot + m_prev - m_new), s_new=jnp.exp(m_loc - m_new))

    def ml_first(p):
        h = p["h"]
        v = proj_ref[:, 2 * GW + h * HD:2 * GW + (h + 1) * HD]
        p["qk"] = _dot_nt(p["q_b"], p["k_b"])
        p["qc"] = _dot(p["q_b"], c_ref[h].astype(BF16))
        p["kv"] = _dot(p["kw_t"], v)

    def ml_second(p):
        h = p["h"]
        qs = slice(h * HD, (h + 1) * HD)
        v = proj_ref[:, 2 * GW + h * HD:2 * GW + (h + 1) * HD]
        o_gate = proj_ref[:, 3 * GW + h * HD:3 * GW + (h + 1) * HD].astype(F32)
        n_prev = n_ref[h]
        w = p["qk"] * p["decay"]
        num = _dot(w.astype(BF16), v) + p["s_inter"] * p["qc"]
        den = jnp.sum(w, axis=1, keepdims=True) + p["s_inter"] * jnp.sum(p["q"] * n_prev, axis=1, keepdims=True)
        hid = num / jnp.maximum(jnp.abs(den), p["floor"])
        c_ref[h] = p["s_old"] * c_ref[h] + p["s_new"] * p["kv"]
        n_ref[h] = p["s_old"] * n_prev + p["s_new"] * p["kw_sum"]
        m_ref[h] = p["m_new"]
        hid = hid * lax.rsqrt(jnp.mean(hid * hid, axis=1, keepdims=True) + EPS) * gml_ref[:, qs]
        y_ref[:, qs] = (hid * jax.nn.sigmoid(o_gate)).astype(y_ref.dtype)

    def ret_operands(h):
        base = 4 * GW + h * HD
        log_gamma = float(np.log(np.float32(1.0) - np.float32(2.0) ** np.float32(-5.0 - 2.0 * h)))
        q = rotary(proj_ref[:, base:base + HD].astype(F32))
        k = rotary(proj_ref[:, base + GW:base + GW + HD].astype(F32)) * (HD ** -0.5)
        decay_q = jnp.exp((pos_col + 1.0) * log_gamma)
        decay_k = jnp.exp((L - 1.0 - pos_col) * log_gamma)
        return dict(
            h=h, q_b=q.astype(BF16), k_b=k.astype(BF16), qd_b=(q * decay_q).astype(BF16),
            kd_t=(k * decay_k).T.astype(BF16),
            decay=jnp.where(causal, jnp.exp(jnp.maximum(diff, 0.0) * log_gamma), 0.0),
            decay_chunk=math.exp(L * log_gamma))

    def ret_first(p):
        h = p["h"]
        v = proj_ref[:, 6 * GW + h * HD:6 * GW + (h + 1) * HD]
        p["qk"] = _dot_nt(p["q_b"], p["k_b"])
        p["qr"] = _dot(p["qd_b"], r_ref[h].astype(BF16))
        p["kv"] = _dot(p["kd_t"], v)

    def ret_second(p):
        h = p["h"]
        v = proj_ref[:, 6 * GW + h * HD:6 * GW + (h + 1) * HD]
        r_gate = proj_ref[:, 7 * GW + h * HD:7 * GW + (h + 1) * HD].astype(F32)
        out = _dot((p["qk"] * p["decay"]).astype(BF16), v) + p["qr"]
        r_ref[h] = p["decay_chunk"] * r_ref[h] + p["kv"]
        out = out - jnp.mean(out, axis=1, keepdims=True)
        out = out * lax.rsqrt(jnp.mean(out * out, axis=1, keepdims=True) + EPS) * gret_ref[:, h * HD:(h + 1) * HD]
        y_ref[:, GW + h * HD:GW + (h + 1) * HD] = (out * (r_gate * jax.nn.sigmoid(r_gate))).astype(y_ref.dtype)

    ml = [ml_operands(h) for h in range(ML_HEADS)]
    ret = [ret_operands(h) for h in range(RET_HEADS)]
    for p in ml:
        ml_first(p)
    for p in ret:
        ret_first(p)
    for p in ml:
        ml_second(p)
    for p in ret:
        ret_second(p)


def mixer0(proj, gates, b_gates, w_conv, g_ml, g_ret, cos, sin):
    bsz, seq, width = proj.shape
    hd = HEAD_DIM0
    return pl.pallas_call(
        _mixer0_kernel,
        grid=(bsz, seq // CHUNK),
        in_specs=[
            pl.BlockSpec((None, CHUNK, width), lambda b, c: (b, c, 0)),
            pl.BlockSpec((None, CHUNK, LANES), lambda b, c: (b, c, 0)),
            pl.BlockSpec((1, LANES), lambda b, c: (0, 0)),
            pl.BlockSpec((CONV_WIDTH, 2 * GROUP_WIDTH), lambda b, c: (0, 0)),
            pl.BlockSpec((1, GROUP_WIDTH), lambda b, c: (0, 0)),
            pl.BlockSpec((1, GROUP_WIDTH), lambda b, c: (0, 0)),
            pl.BlockSpec((CHUNK, hd // 2), lambda b, c: (c, 0)),
            pl.BlockSpec((CHUNK, hd // 2), lambda b, c: (c, 0)),
        ],
        out_specs=pl.BlockSpec((None, CHUNK, 2 * GROUP_WIDTH), lambda b, c: (b, c, 0)),
        out_shape=jax.ShapeDtypeStruct((bsz, seq, 2 * GROUP_WIDTH), BF16),
        scratch_shapes=[
            pltpu.VMEM((ML_HEADS, hd, hd), F32),
            pltpu.VMEM((ML_HEADS, 1, hd), F32),
            pltpu.VMEM((ML_HEADS, 1, 1), F32),
            pltpu.VMEM((RET_HEADS, hd, hd), F32),
            pltpu.VMEM((TAIL_ROWS, 2 * GROUP_WIDTH), BF16),
        ],
        compiler_params=_params(("parallel", "arbitrary")),
        name="mixer0",
    )(proj, gates, b_gates, w_conv, g_ml, g_ret, cos, sin)


SB_BLOCK = 256
SB_GROUP = 4
SB_LOG_ZERO = -110.0
SB_BOUND_SLACK = 1.001


def _sb_kernel(q_ref, k_ref, v_ref, o_ref, kmax_ref):
    T, HD = SB_BLOCK, SB_HEAD_DIM
    qi = pl.program_id(2)
    heads = [slice(g * HD, (g + 1) * HD) for g in range(SB_GROUP)]

    @pl.when(qi == 0)
    def _():
        for g, hs in enumerate(heads):
            kf = k_ref[:, hs].astype(F32)
            ksq = jnp.max(jnp.sum(kf * kf, axis=1, keepdims=True), axis=0, keepdims=True)
            kmax_ref[g] = jnp.sqrt(ksq)

    row = lax.broadcasted_iota(jnp.int32, (T, T), 0)
    col = lax.broadcasted_iota(jnp.int32, (T, T), 1)
    ones_incl = jnp.where(row >= col, 1.0, 0.0).astype(BF16)
    cum_rhs = jnp.concatenate([ones_incl, ones_incl], axis=0)
    strict = col < row

    qs, zmax = [], []
    for g, hs in enumerate(heads):
        q = (q_ref[:, hs].astype(F32) * (HD ** -0.5)).astype(BF16)
        qf = q.astype(F32)
        qnorm = jnp.sqrt(jnp.sum(qf * qf, axis=1, keepdims=True))
        qs.append(q)
        zmax.append(qnorm * kmax_ref[g] * SB_BOUND_SLACK)

    def span(start, nsub, accs, rems, keep):
        zs = [_dot_nt(qs[g], k_ref[pl.ds(start, nsub * T), hs]) for g, hs in enumerate(heads)]
        cums = []
        for g in range(SB_GROUP):
            z = zs[g]
            log_keep = -(jnp.maximum(z, 0.0) + jnp.log(1.0 + jnp.exp(-jnp.abs(z))))
            split = []
            for u in range(nsub):
                part = log_keep[:, u * T:(u + 1) * T]
                if keep[u] is not None:
                    part = jnp.where(keep[u], part, 0.0)
                hi = part.astype(BF16)
                lo = (part - hi.astype(F32)).astype(BF16)
                split.append(jnp.concatenate([hi, lo], axis=1))
            cums.append(_dot(jnp.concatenate(split, axis=0), cum_rhs))
        new_accs, new_rems = [], []
        for g, hs in enumerate(heads):
            rem = rems[g]
            weights = [None] * nsub
            for u in reversed(range(nsub)):
                cum_u = cums[g][u * T:(u + 1) * T]
                a = jnp.exp(zs[g][:, u * T:(u + 1) * T] + cum_u + rem)
                if keep[u] is not None:
                    a = jnp.where(keep[u], a, 0.0)
                weights[u] = a.astype(BF16)
                rem = rem + cum_u[:, 0:1]
            v = v_ref[pl.ds(start, nsub * T), hs]
            new_accs.append(accs[g] + _dot(jnp.concatenate(weights, axis=1), v))
            new_rems.append(rem)
        return tuple(new_accs), tuple(new_rems)

    def live(rems):
        worst = rems[0] + zmax[0]
        for g in range(1, SB_GROUP):
            worst = jnp.maximum(worst, rems[g] + zmax[g])
        return jnp.max(worst) >= SB_LOG_ZERO

    def zeros():
        return (tuple(jnp.zeros((T, HD), F32) for _ in heads), tuple(jnp.zeros((T, 1), F32) for _ in heads))

    def finish(accs):
        for g, hs in enumerate(heads):
            o_ref[:, hs] = accs[g].astype(o_ref.dtype)

    @pl.when(qi == 0)
    def _():
        accs, _ = span(0, 1, *zeros(), [strict])
        finish(accs)

    @pl.when(qi > 0)
    def _():
        accs, rems = span(pl.multiple_of((qi - 1) * T, T), 2, *zeros(), [None, strict])

        def cond(carry):
            return jnp.logical_and(carry[0] >= -1, carry[1])

        def body(carry):
            first, _, accs, rems = carry
            start = pl.multiple_of(jnp.maximum(first, 0) * T, T)
            accs, rems = span(start, 2, accs, rems, [None, first >= 0])
            return first - 2, live(rems), accs, rems

        _, _, accs, _ = lax.while_loop(cond, body, (qi - 3, live(rems), accs, rems))
        finish(accs)


def stick_breaking(qkv):
    bsz, seq, _ = qkv.shape
    T = SB_BLOCK
    width = SB_GROUP * SB_HEAD_DIM
    groups = SB_HEADS // SB_GROUP
    return pl.pallas_call(
        _sb_kernel,
        grid=(bsz, groups, seq // T),
        in_specs=[
            pl.BlockSpec((None, T, width), lambda b, h, i: (b, i, h)),
            pl.BlockSpec((None, seq, width), lambda b, h, i: (b, 0, groups + h)),
            pl.BlockSpec((None, seq, width), lambda b, h, i: (b, 0, 2 * groups + h)),
        ],
        out_specs=pl.BlockSpec((None, T, width), lambda b, h, i: (b, i, h)),
        out_shape=jax.ShapeDtypeStruct((bsz, seq, D_MODEL), BF16),
        scratch_shapes=[pltpu.VMEM((SB_GROUP, 1, 1), F32)],
        compiler_params=_params(("parallel", "parallel", "arbitrary")),
        name="stick_breaking",
    )(qkv, qkv, qkv)


def _rotary_tables(seq):
    half = HEAD_DIM0 // 2
    inv = ROPE_BASE ** (-jnp.arange(0, HEAD_DIM0, 2, dtype=F32) / HEAD_DIM0)
    ang = jnp.arange(seq, dtype=F32)[:, None] * inv[None, :]
    assert ang.shape == (seq, half)
    return jnp.cos(ang), jnp.sin(ang)


def kernel(x, norm_mix0, w_in0, b_gates0, w_conv0, g_ml0, g_ret0, w_out0, norm_ffn0, w_gu0, w_down0,
           norm_mix1, w_qkv1, w_out1, norm_ffn1, w_gu1, w_down1, final_norm):
    bsz, seq, d = x.shape
    m = bsz * seq
    row = lambda t: t.reshape(1, -1).astype(F32)
    n_main = 8 * GROUP_WIDTH
    n_gates = 2 * ML_HEADS

    xf = x.reshape(m, d)

    w_in = w_in0.astype(BF16)
    w_gates = jnp.pad(w_in[:, n_main:], ((0, 0), (0, LANES - n_gates)))
    b_gates = jnp.pad(b_gates0.astype(F32), (0, LANES - n_gates)).reshape(1, LANES)
    proj, gates = norm_proj(xf, row(norm_mix0), w_in, w_gates, n=n_main)
    cos, sin = _rotary_tables(seq)
    y = mixer0(proj.reshape(bsz, seq, n_main), gates.reshape(bsz, seq, LANES), b_gates,
               w_conv0.astype(F32), row(g_ml0), row(g_ret0), cos, sin)
    xf = proj_residual(y.reshape(m, d), w_out0.astype(BF16), xf)
    xf = ffn_residual(xf, row(norm_ffn0), w_gu0, w_down0, row(final_norm), final_norm=False)

    qkv = norm_proj(xf, row(norm_mix1), w_qkv1.astype(BF16))
    o = stick_breaking(qkv.reshape(bsz, seq, 3 * d))
    xf = proj_residual(o.reshape(m, d), w_out1.astype(BF16), xf)
    xf = ffn_residual(xf, row(norm_ffn1), w_gu1, w_down1, row(final_norm), final_norm=True)
    return xf.reshape(bsz, seq, d)
```

```python
import functools
import math

import numpy as np
import jax
import jax.numpy as jnp
from jax import lax
from jax.experimental import pallas as pl
from jax.experimental.pallas import tpu as pltpu

F32 = jnp.float32
BF16 = jnp.bfloat16

D_MODEL = 2048
GROUP_WIDTH = 1024
ML_HEADS = 4
RET_HEADS = 4
HEAD_DIM0 = 256
SB_HEADS = 16
SB_HEAD_DIM = 128
CHUNK = 128
CONV_WIDTH = 4
FFN_HIDDEN = 5632
ROPE_BASE = 10000.0
EPS = 1e-6
LANES = 128
TAIL_ROWS = 16
NEG_BIG = -1e30

VMEM_LIMIT = 60 * 1024 * 1024


def _params(semantics):
    return pltpu.CompilerParams(dimension_semantics=semantics, vmem_limit_bytes=VMEM_LIMIT)


def _rms_norm_rows(x, g):
    return x * lax.rsqrt(jnp.mean(x * x, axis=-1, keepdims=True) + EPS) * g


def _dot(a, b):
    return jnp.dot(a, b, preferred_element_type=F32)


def _dot_nt(a, b):
    return lax.dot_general(a, b, (((1,), (1,)), ((), ())), preferred_element_type=F32)


def _norm_proj_kernel(x_ref, g_ref, w_ref, o_ref, xn_ref):
    @pl.when(pl.program_id(1) == 0)
    def _():
        xn_ref[...] = _rms_norm_rows(x_ref[...], g_ref[...]).astype(BF16)

    o_ref[...] = _dot(xn_ref[...], w_ref[...]).astype(o_ref.dtype)


def _norm_proj_gates_kernel(x_ref, g_ref, w_ref, wg_ref, o_ref, gates_ref, xn_ref):
    @pl.when(pl.program_id(1) == 0)
    def _():
        xn = _rms_norm_rows(x_ref[...], g_ref[...]).astype(BF16)
        xn_ref[...] = xn
        gates_ref[...] = _dot(xn, wg_ref[...])

    o_ref[...] = _dot(xn_ref[...], w_ref[...]).astype(o_ref.dtype)


def norm_proj(x, g, w, w_gates=None, *, n=None, tm=1024, tn=2048):
    m, d = x.shape
    n = w.shape[1] if n is None else n
    grid = (m // tm, n // tn)
    x_spec = pl.BlockSpec((tm, d), lambda i, j: (i, 0))
    g_spec = pl.BlockSpec((1, d), lambda i, j: (0, 0))
    w_spec = pl.BlockSpec((d, tn), lambda i, j: (0, j))
    o_spec = pl.BlockSpec((tm, tn), lambda i, j: (i, j))
    scratch = [pltpu.VMEM((tm, d), BF16)]
    if w_gates is None:
        return pl.pallas_call(
            _norm_proj_kernel,
            grid=grid,
            in_specs=[x_spec, g_spec, w_spec],
            out_specs=o_spec,
            out_shape=jax.ShapeDtypeStruct((m, n), BF16),
            scratch_shapes=scratch,
            compiler_params=_params(("parallel", "arbitrary")),
            name="norm_proj",
        )(x, g, w)
    ng = w_gates.shape[1]
    return pl.pallas_call(
        _norm_proj_gates_kernel,
        grid=grid,
        in_specs=[x_spec, g_spec, w_spec, pl.BlockSpec((d, ng), lambda i, j: (0, 0))],
        out_specs=[o_spec, pl.BlockSpec((tm, ng), lambda i, j: (i, 0))],
        out_shape=[jax.ShapeDtypeStruct((m, n), BF16), jax.ShapeDtypeStruct((m, ng), F32)],
        scratch_shapes=scratch,
        compiler_params=_params(("parallel", "arbitrary")),
        name="norm_proj_gates",
    )(x, g, w, w_gates)


def _proj_residual_kernel(a_ref, w_ref, r_ref, o_ref):
    o_ref[...] = r_ref[...] + _dot(a_ref[...], w_ref[...])


def proj_residual(a, w, r, *, tm=512, tn=2048):
    m, k = a.shape
    n = w.shape[1]
    return pl.pallas_call(
        _proj_residual_kernel,
        grid=(m // tm, n // tn),
        in_specs=[
            pl.BlockSpec((tm, k), lambda i, j: (i, 0)),
            pl.BlockSpec((k, tn), lambda i, j: (0, j)),
            pl.BlockSpec((tm, tn), lambda i, j: (i, j)),
        ],
        out_specs=pl.BlockSpec((tm, tn), lambda i, j: (i, j)),
        out_shape=jax.ShapeDtypeStruct((m, n), F32),
        compiler_params=_params(("parallel", "arbitrary")),
        name="proj_residual",
    )(a, w, r)


def _ffn_step(f, last, x_ref, g_ref, wg_ref, wu_ref, wd_ref, gf_ref, o_ref, xn_ref, final_norm):
    @pl.when(f == 0)
    def _():
        x = x_ref[...]
        xn_ref[...] = _rms_norm_rows(x, g_ref[...]).astype(BF16)
        o_ref[...] = x

    xn = xn_ref[...]
    gate = _dot(xn, wg_ref[...])
    up = _dot(xn, wu_ref[...])
    act = (gate * jax.nn.sigmoid(gate) * up).astype(BF16)
    o_ref[...] += _dot(act, wd_ref[...])

    if final_norm:
        @pl.when(f == last)
        def _():
            o_ref[...] = _rms_norm_rows(o_ref[...], gf_ref[...])


def _ffn_head_kernel(x_ref, g_ref, wg_ref, wu_ref, wd_ref, gf_ref, o_ref, wg16_ref, wu16_ref, wd16_ref, xn_ref,
                     *, final_norm):
    wg16_ref[...] = wg_ref[...].astype(BF16)
    wu16_ref[...] = wu_ref[...].astype(BF16)
    wd16_ref[...] = wd_ref[...].astype(BF16)
    _ffn_step(pl.program_id(0), pl.num_programs(0) - 1, x_ref, g_ref, wg16_ref, wu16_ref, wd16_ref, gf_ref, o_ref,
              xn_ref, final_norm)


def _ffn_kernel(x_ref, g_ref, wg_ref, wu_ref, wd_ref, gf_ref, head_hbm_ref, o_ref, xn_ref, head_sem,
                *, final_norm):
    i = pl.program_id(0)
    f = pl.program_id(1)

    @pl.when(jnp.logical_and(i == 0, f == 0))
    def _():
        fetch = pltpu.make_async_copy(head_hbm_ref, o_ref, head_sem)
        fetch.start()
        fetch.wait()

    @pl.when(i > 0)
    def _():
        _ffn_step(f, pl.num_programs(1) - 1, x_ref, g_ref, wg_ref, wu_ref, wd_ref, gf_ref, o_ref, xn_ref, final_norm)


def ffn_residual(x, g, w_gu, w_down, g_final, *, final_norm, tm=1024, tf=512, tf_head=256):
    m, d = x.shape
    hidden = w_down.shape[0]
    nh = hidden // tf_head
    row_spec = pl.BlockSpec((1, d), lambda *_: (0, 0))
    head, wg16, wu16, wd16 = pl.pallas_call(
        functools.partial(_ffn_head_kernel, final_norm=final_norm),
        grid=(nh,),
        in_specs=[
            pl.BlockSpec((tm, d), lambda f: (0, 0), pipeline_mode=pl.Buffered(1)),
            row_spec,
            pl.BlockSpec((d, tf_head), lambda f: (0, f)),
            pl.BlockSpec((d, tf_head), lambda f: (0, f + nh)),
            pl.BlockSpec((tf_head, d), lambda f: (f, 0)),
            row_spec,
        ],
        out_specs=[
            pl.BlockSpec((tm, d), lambda f: (0, 0)),
            pl.BlockSpec((d, tf_head), lambda f: (0, f)),
            pl.BlockSpec((d, tf_head), lambda f: (0, f)),
            pl.BlockSpec((tf_head, d), lambda f: (f, 0)),
        ],
        out_shape=[
            jax.ShapeDtypeStruct((tm, d), F32),
            jax.ShapeDtypeStruct((d, hidden), BF16),
            jax.ShapeDtypeStruct((d, hidden), BF16),
            jax.ShapeDtypeStruct((hidden, d), BF16),
        ],
        scratch_shapes=[pltpu.VMEM((tm, d), BF16)],
        compiler_params=_params(("arbitrary",)),
        name="ffn_head",
    )(x, g, w_gu, w_gu, w_down, g_final)

    nf = hidden // tf
    held = lambda i, f: jnp.where(i == 0, 0, f)
    return pl.pallas_call(
        functools.partial(_ffn_kernel, final_norm=final_norm),
        grid=(m // tm, nf),
        in_specs=[
            pl.BlockSpec((tm, d), lambda i, f: (i, 0)),
            row_spec,
            pl.BlockSpec((d, tf), lambda i, f: (0, held(i, f))),
            pl.BlockSpec((d, tf), lambda i, f: (0, held(i, f))),
            pl.BlockSpec((tf, d), lambda i, f: (held(i, f), 0)),
            row_spec,
            pl.BlockSpec(memory_space=pl.ANY),
        ],
        out_specs=pl.BlockSpec((tm, d), lambda i, f: (i, 0)),
        out_shape=jax.ShapeDtypeStruct((m, d), F32),
        scratch_shapes=[pltpu.VMEM((tm, d), BF16), pltpu.SemaphoreType.DMA(())],
        compiler_params=_params(("arbitrary", "arbitrary")),
        name="ffn_residual",
    )(x, g, wg16, wu16, wd16, g_final, head)


def _shift_matrix(length):
    rows = (CONV_WIDTH - 1) * length
    r = lax.broadcasted_iota(jnp.int32, (rows, 2 * length), 0)
    c = lax.broadcasted_iota(jnp.int32, (rows, 2 * length), 1)
    sel = jnp.zeros((rows, 2 * length), jnp.bool_)
    for s in range(1, CONV_WIDTH):
        in_band = jnp.logical_and(r >= (s - 1) * length, r < s * length)
        sel = jnp.logical_or(sel, jnp.logical_and(in_band, c == r - (s - 1) * length + length - s))
    return jnp.where(sel, 1.0, 0.0).astype(BF16)


def _conv_silu(x_b, tail_b, w, shift):
    length, ch = x_b.shape
    pad = jnp.zeros((length - tail_b.shape[0], ch), BF16)
    shifted = _dot(shift, jnp.concatenate([pad, tail_b, x_b], axis=0))
    acc = x_b.astype(F32) * w[CONV_WIDTH - 1:CONV_WIDTH]
    for s in range(1, CONV_WIDTH):
        acc = acc + shifted[(s - 1) * length:s * length] * w[CONV_WIDTH - 1 - s:CONV_WIDTH - s]
    return acc * jax.nn.sigmoid(acc)


def _log_sigmoid(x):
    return jnp.minimum(x, 0.0) - jnp.log1p(jnp.exp(-jnp.abs(x)))


def _mixer0_kernel(proj_ref, gates_ref, bg_ref, wconv_ref, gml_ref, gret_ref, cos_ref, sin_ref,
                   y_ref, c_ref, n_ref, m_ref, r_ref, tail_ref):
    L, HD, GW = CHUNK, HEAD_DIM0, GROUP_WIDTH

    @pl.when(pl.program_id(1) == 0)
    def _():
        c_ref[...] = jnp.zeros_like(c_ref)
        n_ref[...] = jnp.zeros_like(n_ref)
        m_ref[...] = jnp.zeros_like(m_ref)
        r_ref[...] = jnp.zeros_like(r_ref)
        tail_ref[...] = jnp.zeros_like(tail_ref)

    row = lax.broadcasted_iota(jnp.int32, (L, L), 0)
    col = lax.broadcasted_iota(jnp.int32, (L, L), 1)
    causal = col <= row
    diff = (row - col).astype(F32)

    gates = gates_ref[...] + bg_ref[...]
    gates_t = gates.T

    cos = cos_ref[...]
    sin = sin_ref[...]
    half = HD // 2
    pos_col = lax.broadcasted_iota(jnp.int32, (L, 1), 0).astype(F32)
    shift = _shift_matrix(L)

    def rotary(t):
        t1, t2 = t[:, :half], t[:, half:]
        return jnp.concatenate([t1 * cos - t2 * sin, t1 * sin + t2 * cos], axis=1)


    def ml_operands(h):
        qs = slice(h * HD, (h + 1) * HD)
        ks = slice(GW + h * HD, GW + (h + 1) * HD)
        q_raw = proj_ref[:, qs]
        k_raw = proj_ref[:, ks]
        q = _conv_silu(q_raw, tail_ref[:, qs], wconv_ref[:, qs], shift) * (HD ** -0.5)
        k = _conv_silu(k_raw, tail_ref[:, ks], wconv_ref[:, ks], shift)
        tail_ref[:, qs] = q_raw[L - TAIL_ROWS:]
        tail_ref[:, ks] = k_raw[L - TAIL_ROWS:]

        ig_col = gates[:, h:h + 1]
        lf_col = _log_sigmoid(gates[:, ML_HEADS + h:ML_HEADS + h + 1])
        ig_row = gates_t[h:h + 1, :]
        lf_row = _log_sigmoid(gates_t[ML_HEADS + h:ML_HEADS + h + 1, :])
        b_col = jnp.sum(jnp.where(causal, lf_row, 0.0), axis=1, keepdims=True)
        b_row = jnp.sum(jnp.where(row <= col, lf_col, 0.0), axis=0, keepdims=True)
        g_tot = jnp.sum(lf_row, axis=1, keepdims=True)
        a_col = g_tot - b_col + ig_col
        m_loc = jnp.max(a_col, axis=0, keepdims=True)
        w_end = jnp.exp(a_col - m_loc)

        m_prev = m_ref[h]
        log_d = jnp.where(causal, b_col - b_row + ig_row, NEG_BIG)
        m_inter = b_col + m_prev
        m_t = jnp.maximum(m_inter, jnp.max(log_d, axis=1, keepdims=True))
        m_new = jnp.maximum(g_tot + m_prev, m_loc)
        kw = k * w_end
        return dict(
            h=h, q=q, q_b=q.astype(BF16), k_b=k.astype(BF16), kw_t=kw.T.astype(BF16),
            kw_sum=jnp.sum(kw, axis=0, keepdims=True), decay=jnp.exp(log_d - m_t),
            s_inter=jnp.exp(m_inter - m_t), floor=jnp.exp(-m_t), m_new=m_new,
            s_old=jnp.exp(g_tot + m_prev - m_new), s_new=jnp.exp(m_loc - m_new))

    def ml_first(p):
        h = p["h"]
        v = proj_ref[:, 2 * GW + h * HD:2 * GW + (h + 1) * HD]
        p["qk"] = _dot_nt(p["q_b"], p["k_b"])
        p["qc"] = _dot(p["q_b"], c_ref[h].astype(BF16))
        p["kv"] = _dot(p["kw_t"], v)

    def ml_second(p):
        h = p["h"]
        qs = slice(h * HD, (h + 1) * HD)
        v = proj_ref[:, 2 * GW + h * HD:2 * GW + (h + 1) * HD]
        o_gate = proj_ref[:, 3 * GW + h * HD:3 * GW + (h + 1) * HD].astype(F32)
        n_prev = n_ref[h]
        w = p["qk"] * p["decay"]
        num = _dot(w.astype(BF16), v) + p["s_inter"] * p["qc"]
        den = jnp.sum(w, axis=1, keepdims=True) + p["s_inter"] * jnp.sum(p["q"] * n_prev, axis=1, keepdims=True)
        hid = num / jnp.maximum(jnp.abs(den), p["floor"])
        c_ref[h] = p["s_old"] * c_ref[h] + p["s_new"] * p["kv"]
        n_ref[h] = p["s_old"] * n_prev + p["s_new"] * p["kw_sum"]
        m_ref[h] = p["m_new"]
        hid = hid * lax.rsqrt(jnp.mean(hid * hid, axis=1, keepdims=True) + EPS) * gml_ref[:, qs]
        y_ref[:, qs] = (hid * jax.nn.sigmoid(o_gate)).astype(y_ref.dtype)

    def ret_operands(h):
        base = 4 * GW + h * HD
        log_gamma = float(np.log(np.float32(1.0) - np.float32(2.0) ** np.float32(-5.0 - 2.0 * h)))
        q = rotary(proj_ref[:, base:base + HD].astype(F32))
        k = rotary(proj_ref[:, base + GW:base + GW + HD].astype(F32)) * (HD ** -0.5)
        decay_q = jnp.exp((pos_col + 1.0) * log_gamma)
        decay_k = jnp.exp((L - 1.0 - pos_col) * log_gamma)
        return dict(
            h=h, q_b=q.astype(BF16), k_b=k.astype(BF16), qd_b=(q * decay_q).astype(BF16),
            kd_t=(k * decay_k).T.astype(BF16),
            decay=jnp.where(causal, jnp.exp(jnp.maximum(diff, 0.0) * log_gamma), 0.0),
            decay_chunk=math.exp(L * log_gamma))

    def ret_first(p):
        h = p["h"]
        v = proj_ref[:, 6 * GW + h * HD:6 * GW + (h + 1) * HD]
        p["qk"] = _dot_nt(p["q_b"], p["k_b"])
        p["qr"] = _dot(p["qd_b"], r_ref[h].astype(BF16))
        p["kv"] = _dot(p["kd_t"], v)

    def ret_second(p):
        h = p["h"]
        v = proj_ref[:, 6 * GW + h * HD:6 * GW + (h + 1) * HD]
        r_gate = proj_ref[:, 7 * GW + h * HD:7 * GW + (h + 1) * HD].astype(F32)
        out = _dot((p["qk"] * p["decay"]).astype(BF16), v) + p["qr"]
        r_ref[h] = p["decay_chunk"] * r_ref[h] + p["kv"]
        out = out - jnp.mean(out, axis=1, keepdims=True)
        out = out * lax.rsqrt(jnp.mean(out * out, axis=1, keepdims=True) + EPS) * gret_ref[:, h * HD:(h + 1) * HD]
        y_ref[:, GW + h * HD:GW + (h + 1) * HD] = (out * (r_gate * jax.nn.sigmoid(r_gate))).astype(y_ref.dtype)

    ml = [ml_operands(h) for h in range(ML_HEADS)]
    ret = [ret_operands(h) for h in range(RET_HEADS)]
    for p in ml:
        ml_first(p)
    for p in ret:
        ret_first(p)
    for p in ml:
        ml_second(p)
    for p in ret:
        ret_second(p)


def mixer0(proj, gates, b_gates, w_conv, g_ml, g_ret, cos, sin):
    bsz, seq, width = proj.shape
    hd = HEAD_DIM0
    return pl.pallas_call(
        _mixer0_kernel,
        grid=(bsz, seq // CHUNK),
        in_specs=[
            pl.BlockSpec((None, CHUNK, width), lambda b, c: (b, c, 0)),
            pl.BlockSpec((None, CHUNK, LANES), lambda b, c: (b, c, 0)),
            pl.BlockSpec((1, LANES), lambda b, c: (0, 0)),
            pl.BlockSpec((CONV_WIDTH, 2 * GROUP_WIDTH), lambda b, c: (0, 0)),
            pl.BlockSpec((1, GROUP_WIDTH), lambda b, c: (0, 0)),
            pl.BlockSpec((1, GROUP_WIDTH), lambda b, c: (0, 0)),
            pl.BlockSpec((CHUNK, hd // 2), lambda b, c: (c, 0)),
            pl.BlockSpec((CHUNK, hd // 2), lambda b, c: (c, 0)),
        ],
        out_specs=pl.BlockSpec((None, CHUNK, 2 * GROUP_WIDTH), lambda b, c: (b, c, 0)),
        out_shape=jax.ShapeDtypeStruct((bsz, seq, 2 * GROUP_WIDTH), BF16),
        scratch_shapes=[
            pltpu.VMEM((ML_HEADS, hd, hd), F32),
            pltpu.VMEM((ML_HEADS, 1, hd), F32),
            pltpu.VMEM((ML_HEADS, 1, 1), F32),
            pltpu.VMEM((RET_HEADS, hd, hd), F32),
            pltpu.VMEM((TAIL_ROWS, 2 * GROUP_WIDTH), BF16),
        ],
        compiler_params=_params(("parallel", "arbitrary")),
        name="mixer0",
    )(proj, gates, b_gates, w_conv, g_ml, g_ret, cos, sin)


SB_BLOCK = 256
SB_GROUP = 4
SB_LOG_ZERO = -110.0
SB_BOUND_SLACK = 1.001


def _sb_kernel(q_ref, k_ref, v_ref, o_ref, kmax_ref):
    T, HD = SB_BLOCK, SB_HEAD_DIM
    qi = pl.program_id(2)
    heads = [slice(g * HD, (g + 1) * HD) for g in range(SB_GROUP)]

    @pl.when(qi == 0)
    def _():
        for g, hs in enumerate(heads):
            kf = k_ref[:, hs].astype(F32)
            ksq = jnp.max(jnp.sum(kf * kf, axis=1, keepdims=True), axis=0, keepdims=True)
            kmax_ref[g] = jnp.sqrt(ksq)

    row = lax.broadcasted_iota(jnp.int32, (T, T), 0)
    col = lax.broadcasted_iota(jnp.int32, (T, T), 1)
    ones_incl = jnp.where(row >= col, 1.0, 0.0).astype(BF16)
    cum_rhs = jnp.concatenate([ones_incl, ones_incl], axis=0)
    strict = col < row

    qs, zmax = [], []
    for g, hs in enumerate(heads):
        q = (q_ref[:, hs].astype(F32) * (HD ** -0.5)).astype(BF16)
        qf = q.astype(F32)
        qnorm = jnp.sqrt(jnp.sum(qf * qf, axis=1, keepdims=True))
        qs.append(q)
        zmax.append(qnorm * kmax_ref[g] * SB_BOUND_SLACK)

    def span(start, nsub, accs, rems, keep):
        zs = [_dot_nt(qs[g], k_ref[pl.ds(start, nsub * T), hs]) for g, hs in enumerate(heads)]
        cums = []
        for g in range(SB_GROUP):
            z = zs[g]
            log_keep = -(jnp.maximum(z, 0.0) + jnp.log(1.0 + jnp.exp(-jnp.abs(z))))
            split = []
            for u in range(nsub):
                part = log_keep[:, u * T:(u + 1) * T]
                if keep[u] is not None:
                    part = jnp.where(keep[u], part, 0.0)
                hi = part.astype(BF16)
                lo = (part - hi.astype(F32)).astype(BF16)
                split.append(jnp.concatenate([hi, lo], axis=1))
            cums.append(_dot(jnp.concatenate(split, axis=0), cum_rhs))
        new_accs, new_rems = [], []
        for g, hs in enumerate(heads):
            rem = rems[g]
            weights = [None] * nsub
            for u in reversed(range(nsub)):
                cum_u = cums[g][u * T:(u + 1) * T]
                a = jnp.exp(zs[g][:, u * T:(u + 1) * T] + cum_u + rem)
                if keep[u] is not None:
                    a = jnp.where(keep[u], a, 0.0)
                weights[u] = a.astype(BF16)
                rem = rem + cum_u[:, 0:1]
            v = v_ref[pl.ds(start, nsub * T), hs]
            new_accs.append(accs[g] + _dot(jnp.concatenate(weights, axis=1), v))
            new_rems.append(rem)
        return tuple(new_accs), tuple(new_rems)

    def live(rems):
        worst = rems[0] + zmax[0]
        for g in range(1, SB_GROUP):
            worst = jnp.maximum(worst, rems[g] + zmax[g])
        return jnp.max(worst) >= SB_LOG_ZERO

    def zeros():
        return (tuple(jnp.zeros((T, HD), F32) for _ in heads), tuple(jnp.zeros((T, 1), F32) for _ in heads))

    def finish(accs):
        for g, hs in enumerate(heads):
            o_ref[:, hs] = accs[g].astype(o_ref.dtype)

    @pl.when(qi == 0)
    def _():
        accs, _ = span(0, 1, *zeros(), [strict])
        finish(accs)

    @pl.when(qi > 0)
    def _():
        accs, rems = span(pl.multiple_of((qi - 1) * T, T), 2, *zeros(), [None, strict])

        def cond(carry):
            return jnp.logical_and(carry[0] >= -1, carry[1])

        def body(carry):
            first, _, accs, rems = carry
            start = pl.multiple_of(jnp.maximum(first, 0) * T, T)
            accs, rems = span(start, 2, accs, rems, [None, first >= 0])
            return first - 2, live(rems), accs, rems

        _, _, accs, _ = lax.while_loop(cond, body, (qi - 3, live(rems), accs, rems))
        finish(accs)


def stick_breaking(qkv):
    bsz, seq, _ = qkv.shape
    T = SB_BLOCK
    width = SB_GROUP * SB_HEAD_DIM
    groups = SB_HEADS // SB_GROUP
    return pl.pallas_call(
        _sb_kernel,
        grid=(bsz, groups, seq // T),
        in_specs=[
            pl.BlockSpec((None, T, width), lambda b, h, i: (b, i, h)),
            pl.BlockSpec((None, seq, width), lambda b, h, i: (b, 0, groups + h)),
            pl.BlockSpec((None, seq, width), lambda b, h, i: (b, 0, 2 * groups + h)),
        ],
        out_specs=pl.BlockSpec((None, T, width), lambda b, h, i: (b, i, h)),
        out_shape=jax.ShapeDtypeStruct((bsz, seq, D_MODEL), BF16),
        scratch_shapes=[pltpu.VMEM((SB_GROUP, 1, 1), F32)],
        compiler_params=_params(("parallel", "parallel", "arbitrary")),
        name="stick_breaking",
    )(qkv, qkv, qkv)


def _rotary_tables(seq):
    half = HEAD_DIM0 // 2
    inv = ROPE_BASE ** (-jnp.arange(0, HEAD_DIM0, 2, dtype=F32) / HEAD_DIM0)
    ang = jnp.arange(seq, dtype=F32)[:, None] * inv[None, :]
    assert ang.shape == (seq, half)
    return jnp.cos(ang), jnp.sin(ang)


def kernel(x, norm_mix0, w_in0, b_gates0, w_conv0, g_ml0, g_ret0, w_out0, norm_ffn0, w_gu0, w_down0,
           norm_mix1, w_qkv1, w_out1, norm_ffn1, w_gu1, w_down1, final_norm):
    bsz, seq, d = x.shape
    m = bsz * seq
    row = lambda t: t.reshape(1, -1).astype(F32)
    n_main = 8 * GROUP_WIDTH
    n_gates = 2 * ML_HEADS

    xf = x.reshape(m, d)

    w_in = w_in0.astype(BF16)
    w_gates = jnp.pad(w_in[:, n_main:], ((0, 0), (0, LANES - n_gates)))
    b_gates = jnp.pad(b_gates0.astype(F32), (0, LANES - n_gates)).reshape(1, LANES)
    proj, gates = norm_proj(xf, row(norm_mix0), w_in, w_gates, n=n_main)
    cos, sin = _rotary_tables(seq)
    y = mixer0(proj.reshape(bsz, seq, n_main), gates.reshape(bsz, seq, LANES), b_gates,
               w_conv0.astype(F32), row(g_ml0), row(g_ret0), cos, sin)
    xf = proj_residual(y.reshape(m, d), w_out0.astype(BF16), xf)
    xf = ffn_residual(xf, row(norm_ffn0), w_gu0, w_down0, row(final_norm), final_norm=False)

    qkv = norm_proj(xf, row(norm_mix1), w_qkv1.astype(BF16))
    o = stick_breaking(qkv.reshape(bsz, seq, 3 * d))
    xf = proj_residual(o.reshape(m, d), w_out1.astype(BF16), xf)
    xf = ffn_residual(xf, row(norm_ffn1), w_gu1, w_down1, row(final_norm), final_norm=True)
    return xf.reshape(bsz, seq, d)
```

```python
import functools
import math

import numpy as np
import jax
import jax.numpy as jnp
from jax import lax
from jax.experimental import pallas as pl
from jax.experimental.pallas import tpu as pltpu

F32 = jnp.float32
BF16 = jnp.bfloat16

D_MODEL = 2048
GROUP_WIDTH = 1024
ML_HEADS = 4
RET_HEADS = 4
HEAD_DIM0 = 256
SB_HEADS = 16
SB_HEAD_DIM = 128
CHUNK = 128
CONV_WIDTH = 4
FFN_HIDDEN = 5632
ROPE_BASE = 10000.0
EPS = 1e-6
LANES = 128
TAIL_ROWS = 16
NEG_BIG = -1e30

VMEM_LIMIT = 60 * 1024 * 1024


def _params(semantics):
    return pltpu.CompilerParams(dimension_semantics=semantics, vmem_limit_bytes=VMEM_LIMIT)


def _rms_norm_rows(x, g):
    return x * lax.rsqrt(jnp.mean(x * x, axis=-1, keepdims=True) + EPS) * g


def _dot(a, b):
    return jnp.dot(a, b, preferred_element_type=F32)


def _dot_nt(a, b):
    return lax.dot_general(a, b, (((1,), (1,)), ((), ())), preferred_element_type=F32)


def _cast_specs(weights, block_rows, step_of):
    in_specs, out_shapes = [], []
    for w, rows in zip(weights, block_rows):
        blocks, rem = divmod(w.shape[0], rows)
        assert rem == 0, (w.shape, rows)
        spec = pl.BlockSpec((rows, w.shape[1]),
                            lambda *ids, last=blocks - 1: (jnp.minimum(step_of(*ids), last), 0))
        in_specs.append(spec)
        out_shapes.append(jax.ShapeDtypeStruct(w.shape, BF16))
    return in_specs, out_shapes


def _cast_blocks(src_refs, dst_refs):
    for src, dst in zip(src_refs, dst_refs):
        dst[...] = src[...].astype(BF16)


def _norm_proj_kernel(*refs, with_gates, n_cast):
    n_in = 3 + with_gates
    x_ref, g_ref, w_ref = refs[:3]
    cast_src = refs[n_in:n_in + n_cast]
    outs = refs[n_in + n_cast:-1]
    o_ref = outs[0]
    cast_dst = outs[1 + with_gates:]
    xn_ref = refs[-1]

    @pl.when(pl.program_id(1) == 0)
    def _():
        xn = _rms_norm_rows(x_ref[...], g_ref[...]).astype(BF16)
        xn_ref[...] = xn
        if with_gates:
            outs[1][...] = _dot(xn, refs[3][...])

    o_ref[...] = _dot(xn_ref[...], w_ref[...]).astype(o_ref.dtype)
    _cast_blocks(cast_src, cast_dst)


def norm_proj(x, g, w, w_gates=None, *, n=None, casts=(), cast_rows=(), tm=1024, tn=2048):
    m, d = x.shape
    n = w.shape[1] if n is None else n
    nj = n // tn
    grid = (m // tm, nj)
    with_gates = w_gates is not None
    cast_specs, cast_shapes = _cast_specs(casts, cast_rows, lambda i, j: i * nj + j)
    in_specs = [
        pl.BlockSpec((tm, d), lambda i, j: (i, 0)),
        pl.BlockSpec((1, d), lambda i, j: (0, 0)),
        pl.BlockSpec((d, tn), lambda i, j: (0, j)),
    ]
    out_specs = [pl.BlockSpec((tm, tn), lambda i, j: (i, j))]
    out_shapes = [jax.ShapeDtypeStruct((m, n), BF16)]
    operands = [x, g, w]
    if with_gates:
        ng = w_gates.shape[1]
        in_specs.append(pl.BlockSpec((d, ng), lambda i, j: (0, 0)))
        out_specs.append(pl.BlockSpec((tm, ng), lambda i, j: (i, 0)))
        out_shapes.append(jax.ShapeDtypeStruct((m, ng), F32))
        operands.append(w_gates)
    return pl.pallas_call(
        functools.partial(_norm_proj_kernel, with_gates=with_gates, n_cast=len(casts)),
        grid=grid,
        in_specs=in_specs + cast_specs,
        out_specs=out_specs + cast_specs,
        out_shape=out_shapes + cast_shapes,
        scratch_shapes=[pltpu.VMEM((tm, d), BF16)],
        compiler_params=_params(("arbitrary", "arbitrary")),
        name="norm_proj_gates" if with_gates else "norm_proj",
    )(*operands, *casts)


def _proj_residual_kernel(a_ref, w_ref, r_ref, o_ref):
    o_ref[...] = r_ref[...] + _dot(a_ref[...], w_ref[...])


def proj_residual(a, w, r, *, tm=512, tn=2048):
    m, k = a.shape
    n = w.shape[1]
    return pl.pallas_call(
        _proj_residual_kernel,
        grid=(m // tm, n // tn),
        in_specs=[
            pl.BlockSpec((tm, k), lambda i, j: (i, 0)),
            pl.BlockSpec((k, tn), lambda i, j: (0, j)),
            pl.BlockSpec((tm, tn), lambda i, j: (i, j)),
        ],
        out_specs=pl.BlockSpec((tm, tn), lambda i, j: (i, j)),
        out_shape=jax.ShapeDtypeStruct((m, n), F32),
        compiler_params=_params(("parallel", "arbitrary")),
        name="proj_residual",
    )(a, w, r)


def _ffn_kernel(*refs, final_norm, n_cast):
    x_ref, g_ref, wg_ref, wu_ref, wd_ref, gf_ref = refs[:6]
    cast_src = refs[6:6 + n_cast]
    o_ref = refs[6 + n_cast]
    cast_dst = refs[7 + n_cast:-1]
    xn_ref = refs[-1]
    f = pl.program_id(1)

    @pl.when(f == 0)
    def _():
        x = x_ref[...]
        xn_ref[...] = _rms_norm_rows(x, g_ref[...]).astype(BF16)
        o_ref[...] = x

    xn = xn_ref[...]
    gate = _dot(xn, wg_ref[...])
    up = _dot(xn, wu_ref[...])
    act = (gate * jax.nn.sigmoid(gate) * up).astype(BF16)
    o_ref[...] += _dot(act, wd_ref[...])

    if final_norm:
        @pl.when(f == pl.num_programs(1) - 1)
        def _():
            o_ref[...] = _rms_norm_rows(o_ref[...], gf_ref[...])

    _cast_blocks(cast_src, cast_dst)


def ffn_residual(x, g, w_gu, w_down, g_final, *, final_norm, casts=(), cast_rows=(), tm=1024, tf=512):
    m, d = x.shape
    hidden = w_down.shape[0]
    nf = hidden // tf
    row_spec = pl.BlockSpec((1, d), lambda i, f: (0, 0))
    cast_specs, cast_shapes = _cast_specs(casts, cast_rows, lambda i, f: i * nf + f)
    return pl.pallas_call(
        functools.partial(_ffn_kernel, final_norm=final_norm, n_cast=len(casts)),
        grid=(m // tm, nf),
        in_specs=[
            pl.BlockSpec((tm, d), lambda i, f: (i, 0)),
            row_spec,
            pl.BlockSpec((d, tf), lambda i, f: (0, f)),
            pl.BlockSpec((d, tf), lambda i, f: (0, f + nf)),
            pl.BlockSpec((tf, d), lambda i, f: (f, 0)),
            row_spec,
        ] + cast_specs,
        out_specs=[pl.BlockSpec((tm, d), lambda i, f: (i, 0))] + cast_specs,
        out_shape=[jax.ShapeDtypeStruct((m, d), F32)] + cast_shapes,
        scratch_shapes=[pltpu.VMEM((tm, d), BF16)],
        compiler_params=_params(("arbitrary", "arbitrary")),
        name="ffn_residual",
    )(x, g, w_gu, w_gu, w_down, g_final, *casts)


def _shift_matrix(length):
    rows = (CONV_WIDTH - 1) * length
    r = lax.broadcasted_iota(jnp.int32, (rows, 2 * length), 0)
    c = lax.broadcasted_iota(jnp.int32, (rows, 2 * length), 1)
    sel = jnp.zeros((rows, 2 * length), jnp.bool_)
    for s in range(1, CONV_WIDTH):
        in_band = jnp.logical_and(r >= (s - 1) * length, r < s * length)
        sel = jnp.logical_or(sel, jnp.logical_and(in_band, c == r - (s - 1) * length + length - s))
    return jnp.where(sel, 1.0, 0.0).astype(BF16)


def _conv_silu(x_b, tail_b, w, shift):
    length, ch = x_b.shape
    pad = jnp.zeros((length - tail_b.shape[0], ch), BF16)
    shifted = _dot(shift, jnp.concatenate([pad, tail_b, x_b], axis=0))
    acc = x_b.astype(F32) * w[CONV_WIDTH - 1:CONV_WIDTH]
    for s in range(1, CONV_WIDTH):
        acc = acc + shifted[(s - 1) * length:s * length] * w[CONV_WIDTH - 1 - s:CONV_WIDTH - s]
    return acc * jax.nn.sigmoid(acc)


def _log_sigmoid(x):
    return jnp.minimum(x, 0.0) - jnp.log1p(jnp.exp(-jnp.abs(x)))


def _mixer0_kernel(*refs, n_cast):
    proj_ref, gates_ref, bg_ref, wconv_ref, gml_ref, gret_ref, cos_ref, sin_ref = refs[:8]
    cast_src = refs[8:8 + n_cast]
    y_ref = refs[8 + n_cast]
    cast_dst = refs[9 + n_cast:9 + 2 * n_cast]
    c_ref, n_ref, m_ref, r_ref, tail_ref = refs[9 + 2 * n_cast:]
    L, HD, GW = CHUNK, HEAD_DIM0, GROUP_WIDTH
    _cast_blocks(cast_src, cast_dst)

    @pl.when(pl.program_id(1) == 0)
    def _():
        c_ref[...] = jnp.zeros_like(c_ref)
        n_ref[...] = jnp.zeros_like(n_ref)
        m_ref[...] = jnp.zeros_like(m_ref)
        r_ref[...] = jnp.zeros_like(r_ref)
        tail_ref[...] = jnp.zeros_like(tail_ref)

    row = lax.broadcasted_iota(jnp.int32, (L, L), 0)
    col = lax.broadcasted_iota(jnp.int32, (L, L), 1)
    causal = col <= row
    diff = (row - col).astype(F32)

    gates = gates_ref[...] + bg_ref[...]
    gates_t = gates.T

    cos = cos_ref[...]
    sin = sin_ref[...]
    half = HD // 2
    pos_col = lax.broadcasted_iota(jnp.int32, (L, 1), 0).astype(F32)
    shift = _shift_matrix(L)

    def rotary(t):
        t1, t2 = t[:, :half], t[:, half:]
        return jnp.concatenate([t1 * cos - t2 * sin, t1 * sin + t2 * cos], axis=1)


    def ml_operands(h):
        qs = slice(h * HD, (h + 1) * HD)
        ks = slice(GW + h * HD, GW + (h + 1) * HD)
        q_raw = proj_ref[:, qs]
        k_raw = proj_ref[:, ks]
        q = _conv_silu(q_raw, tail_ref[:, qs], wconv_ref[:, qs], shift) * (HD ** -0.5)
        k = _conv_silu(k_raw, tail_ref[:, ks], wconv_ref[:, ks], shift)
        tail_ref[:, qs] = q_raw[L - TAIL_ROWS:]
        tail_ref[:, ks] = k_raw[L - TAIL_ROWS:]

        ig_col = gates[:, h:h + 1]
        lf_col = _log_sigmoid(gates[:, ML_HEADS + h:ML_HEADS + h + 1])
        ig_row = gates_t[h:h + 1, :]
        lf_row = _log_sigmoid(gates_t[ML_HEADS + h:ML_HEADS + h + 1, :])
        b_col = jnp.sum(jnp.where(causal, lf_row, 0.0), axis=1, keepdims=True)
        b_row = jnp.sum(jnp.where(row <= col, lf_col, 0.0), axis=0, keepdims=True)
        g_tot = jnp.sum(lf_row, axis=1, keepdims=True)
        a_col = g_tot - b_col + ig_col
        m_loc = jnp.max(a_col, axis=0, keepdims=True)
        w_end = jnp.exp(a_col - m_loc)

        m_prev = m_ref[h]
        log_d = jnp.where(causal, b_col - b_row + ig_row, NEG_BIG)
        m_inter = b_col + m_prev
        m_t = jnp.maximum(m_inter, jnp.max(log_d, axis=1, keepdims=True))
        m_new = jnp.maximum(g_tot + m_prev, m_loc)
        kw = k * w_end
        return dict(
            h=h, q=q, q_b=q.astype(BF16), k_b=k.astype(BF16), kw_t=kw.T.astype(BF16),
            kw_sum=jnp.sum(kw, axis=0, keepdims=True), decay=jnp.exp(log_d - m_t),
            s_inter=jnp.exp(m_inter - m_t), floor=jnp.exp(-m_t), m_new=m_new,
            s_old=jnp.exp(g_tot + m_prev - m_new), s_new=jnp.exp(m_loc - m_new))

    def ml_first(p):
        h = p["h"]
        v = proj_ref[:, 2 * GW + h * HD:2 * GW + (h + 1) * HD]
        p["qk"] = _dot_nt(p["q_b"], p["k_b"])
        p["qc"] = _dot(p["q_b"], c_ref[h].astype(BF16))
        p["kv"] = _dot(p["kw_t"], v)

    def ml_second(p):
        h = p["h"]
        qs = slice(h * HD, (h + 1) * HD)
        v = proj_ref[:, 2 * GW + h * HD:2 * GW + (h + 1) * HD]
        o_gate = proj_ref[:, 3 * GW + h * HD:3 * GW + (h + 1) * HD].astype(F32)
        n_prev = n_ref[h]
        w = p["qk"] * p["decay"]
        num = _dot(w.astype(BF16), v) + p["s_inter"] * p["qc"]
        den = jnp.sum(w, axis=1, keepdims=True) + p["s_inter"] * jnp.sum(p["q"] * n_prev, axis=1, keepdims=True)
        hid = num / jnp.maximum(jnp.abs(den), p["floor"])
        c_ref[h] = p["s_old"] * c_ref[h] + p["s_new"] * p["kv"]
        n_ref[h] = p["s_old"] * n_prev + p["s_new"] * p["kw_sum"]
        m_ref[h] = p["m_new"]
        hid = hid * lax.rsqrt(jnp.mean(hid * hid, axis=1, keepdims=True) + EPS) * gml_ref[:, qs]
        y_ref[:, qs] = (hid * jax.nn.sigmoid(o_gate)).astype(y_ref.dtype)

    def ret_operands(h):
        base = 4 * GW + h * HD
        log_gamma = float(np.log(np.float32(1.0) - np.float32(2.0) ** np.float32(-5.0 - 2.0 * h)))
        q = rotary(proj_ref[:, base:base + HD].astype(F32))
        k = rotary(proj_ref[:, base + GW:base + GW + HD].astype(F32)) * (HD ** -0.5)
        decay_q = jnp.exp((pos_col + 1.0) * log_gamma)
        decay_k = jnp.exp((L - 1.0 - pos_col) * log_gamma)
        return dict(
            h=h, q_b=q.astype(BF16), k_b=k.astype(BF16), qd_b=(q * decay_q).astype(BF16),
            kd_t=(k * decay_k).T.astype(BF16),
            decay=jnp.where(causal, jnp.exp(jnp.maximum(diff, 0.0) * log_gamma), 0.0),
            decay_chunk=math.exp(L * log_gamma))

    def ret_first(p):
        h = p["h"]
        v = proj_ref[:, 6 * GW + h * HD:6 * GW + (h + 1) * HD]
        p["qk"] = _dot_nt(p["q_b"], p["k_b"])
        p["qr"] = _dot(p["qd_b"], r_ref[h].astype(BF16))
        p["kv"] = _dot(p["kd_t"], v)

    def ret_second(p):
        h = p["h"]
        v = proj_ref[:, 6 * GW + h * HD:6 * GW + (h + 1) * HD]
        r_gate = proj_ref[:, 7 * GW + h * HD:7 * GW + (h + 1) * HD].astype(F32)
        out = _dot((p["qk"] * p["decay"]).astype(BF16), v) + p["qr"]
        r_ref[h] = p["decay_chunk"] * r_ref[h] + p["kv"]
        out = out - jnp.mean(out, axis=1, keepdims=True)
        out = out * lax.rsqrt(jnp.mean(out * out, axis=1, keepdims=True) + EPS) * gret_ref[:, h * HD:(h + 1) * HD]
        y_ref[:, GW + h * HD:GW + (h + 1) * HD] = (out * (r_gate * jax.nn.sigmoid(r_gate))).astype(y_ref.dtype)

    ml = [ml_operands(h) for h in range(ML_HEADS)]
    ret = [ret_operands(h) for h in range(RET_HEADS)]
    for p in ml:
        ml_first(p)
    for p in ret:
        ret_first(p)
    for p in ml:
        ml_second(p)
    for p in ret:
        ret_second(p)


def mixer0(proj, gates, b_gates, w_conv, g_ml, g_ret, cos, sin, *, casts=(), cast_rows=()):
    bsz, seq, width = proj.shape
    hd = HEAD_DIM0
    nc = seq // CHUNK
    cast_specs, cast_shapes = _cast_specs(casts, cast_rows, lambda b, c: b * nc + c)
    return pl.pallas_call(
        functools.partial(_mixer0_kernel, n_cast=len(casts)),
        grid=(bsz, nc),
        in_specs=[
            pl.BlockSpec((None, CHUNK, width), lambda b, c: (b, c, 0)),
            pl.BlockSpec((None, CHUNK, LANES), lambda b, c: (b, c, 0)),
            pl.BlockSpec((1, LANES), lambda b, c: (0, 0)),
            pl.BlockSpec((CONV_WIDTH, 2 * GROUP_WIDTH), lambda b, c: (0, 0)),
            pl.BlockSpec((1, GROUP_WIDTH), lambda b, c: (0, 0)),
            pl.BlockSpec((1, GROUP_WIDTH), lambda b, c: (0, 0)),
            pl.BlockSpec((CHUNK, hd // 2), lambda b, c: (c, 0)),
            pl.BlockSpec((CHUNK, hd // 2), lambda b, c: (c, 0)),
        ] + cast_specs,
        out_specs=[pl.BlockSpec((None, CHUNK, 2 * GROUP_WIDTH), lambda b, c: (b, c, 0))] + cast_specs,
        out_shape=[jax.ShapeDtypeStruct((bsz, seq, 2 * GROUP_WIDTH), BF16)] + cast_shapes,
        scratch_shapes=[
            pltpu.VMEM((ML_HEADS, hd, hd), F32),
            pltpu.VMEM((ML_HEADS, 1, hd), F32),
            pltpu.VMEM((ML_HEADS, 1, 1), F32),
            pltpu.VMEM((RET_HEADS, hd, hd), F32),
            pltpu.VMEM((TAIL_ROWS, 2 * GROUP_WIDTH), BF16),
        ],
        compiler_params=_params(("arbitrary", "arbitrary")),
        name="mixer0",
    )(proj, gates, b_gates, w_conv, g_ml, g_ret, cos, sin, *casts)


SB_BLOCK = 256
SB_GROUP = 4
SB_LOG_ZERO = -110.0
SB_BOUND_SLACK = 1.001


def _sb_kernel(q_ref, k_ref, v_ref, o_ref, kmax_ref):
    T, HD = SB_BLOCK, SB_HEAD_DIM
    qi = pl.program_id(2)
    heads = [slice(g * HD, (g + 1) * HD) for g in range(SB_GROUP)]

    @pl.when(qi == 0)
    def _():
        for g, hs in enumerate(heads):
            kf = k_ref[:, hs].astype(F32)
            ksq = jnp.max(jnp.sum(kf * kf, axis=1, keepdims=True), axis=0, keepdims=True)
            kmax_ref[g] = jnp.sqrt(ksq)

    row = lax.broadcasted_iota(jnp.int32, (T, T), 0)
    col = lax.broadcasted_iota(jnp.int32, (T, T), 1)
    ones_incl = jnp.where(row >= col, 1.0, 0.0).astype(BF16)
    cum_rhs = jnp.concatenate([ones_incl, ones_incl], axis=0)
    strict = col < row

    qs, zmax = [], []
    for g, hs in enumerate(heads):
        q = (q_ref[:, hs].astype(F32) * (HD ** -0.5)).astype(BF16)
        qf = q.astype(F32)
        qnorm = jnp.sqrt(jnp.sum(qf * qf, axis=1, keepdims=True))
        qs.append(q)
        zmax.append(qnorm * kmax_ref[g] * SB_BOUND_SLACK)

    def span(start, nsub, accs, rems, keep):
        zs = [_dot_nt(qs[g], k_ref[pl.ds(start, nsub * T), hs]) for g, hs in enumerate(heads)]
        cums = []
        for g in range(SB_GROUP):
            z = zs[g]
            log_keep = -(jnp.maximum(z, 0.0) + jnp.log(1.0 + jnp.exp(-jnp.abs(z))))
            split = []
            for u in range(nsub):
                part = log_keep[:, u * T:(u + 1) * T]
                if keep[u] is not None:
                    part = jnp.where(keep[u], part, 0.0)
                hi = part.astype(BF16)
                lo = (part - hi.astype(F32)).astype(BF16)
                split.append(jnp.concatenate([hi, lo], axis=1))
            cums.append(_dot(jnp.concatenate(split, axis=0), cum_rhs))
        new_accs, new_rems = [], []
        for g, hs in enumerate(heads):
            rem = rems[g]
            weights = [None] * nsub
            for u in reversed(range(nsub)):
                cum_u = cums[g][u * T:(u + 1) * T]
                a = jnp.exp(zs[g][:, u * T:(u + 1) * T] + cum_u + rem)
                if keep[u] is not None:
                    a = jnp.where(keep[u], a, 0.0)
                weights[u] = a.astype(BF16)
                rem = rem + cum_u[:, 0:1]
            v = v_ref[pl.ds(start, nsub * T), hs]
            new_accs.append(accs[g] + _dot(jnp.concatenate(weights, axis=1), v))
            new_rems.append(rem)
        return tuple(new_accs), tuple(new_rems)

    def live(rems):
        worst = rems[0] + zmax[0]
        for g in range(1, SB_GROUP):
            worst = jnp.maximum(worst, rems[g] + zmax[g])
        return jnp.max(worst) >= SB_LOG_ZERO

    def zeros():
        return (tuple(jnp.zeros((T, HD), F32) for _ in heads), tuple(jnp.zeros((T, 1), F32) for _ in heads))

    def finish(accs):
        for g, hs in enumerate(heads):
            o_ref[:, hs] = accs[g].astype(o_ref.dtype)

    @pl.when(qi == 0)
    def _():
        accs, _ = span(0, 1, *zeros(), [strict])
        finish(accs)

    @pl.when(qi > 0)
    def _():
        accs, rems = span(pl.multiple_of((qi - 1) * T, T), 2, *zeros(), [None, strict])

        def cond(carry):
            return jnp.logical_and(carry[0] >= -1, carry[1])

        def body(carry):
            first, _, accs, rems = carry
            start = pl.multiple_of(jnp.maximum(first, 0) * T, T)
            accs, rems = span(start, 2, accs, rems, [None, first >= 0])
            return first - 2, live(rems), accs, rems

        _, _, accs, _ = lax.while_loop(cond, body, (qi - 3, live(rems), accs, rems))
        finish(accs)


def stick_breaking(qkv):
    bsz, seq, _ = qkv.shape
    T = SB_BLOCK
    width = SB_GROUP * SB_HEAD_DIM
    groups = SB_HEADS // SB_GROUP
    return pl.pallas_call(
        _sb_kernel,
        grid=(bsz, groups, seq // T),
        in_specs=[
            pl.BlockSpec((None, T, width), lambda b, h, i: (b, i, h)),
            pl.BlockSpec((None, seq, width), lambda b, h, i: (b, 0, groups + h)),
            pl.BlockSpec((None, seq, width), lambda b, h, i: (b, 0, 2 * groups + h)),
        ],
        out_specs=pl.BlockSpec((None, T, width), lambda b, h, i: (b, i, h)),
        out_shape=jax.ShapeDtypeStruct((bsz, seq, D_MODEL), BF16),
        scratch_shapes=[pltpu.VMEM((SB_GROUP, 1, 1), F32)],
        compiler_params=_params(("parallel", "parallel", "arbitrary")),
        name="stick_breaking",
    )(qkv, qkv, qkv)


def _rotary_tables(seq):
    half = HEAD_DIM0 // 2
    inv = ROPE_BASE ** (-jnp.arange(0, HEAD_DIM0, 2, dtype=F32) / HEAD_DIM0)
    ang = jnp.arange(seq, dtype=F32)[:, None] * inv[None, :]
    assert ang.shape == (seq, half)
    return jnp.cos(ang), jnp.sin(ang)


def kernel(x, norm_mix0, w_in0, b_gates0, w_conv0, g_ml0, g_ret0, w_out0, norm_ffn0, w_gu0, w_down0,
           norm_mix1, w_qkv1, w_out1, norm_ffn1, w_gu1, w_down1, final_norm):
    bsz, seq, d = x.shape
    m = bsz * seq
    row = lambda t: t.reshape(1, -1).astype(F32)
    n_main = 8 * GROUP_WIDTH
    n_gates = 2 * ML_HEADS

    xf = x.reshape(m, d)

    w_in = w_in0.astype(BF16)
    w_gates = jnp.pad(w_in[:, n_main:], ((0, 0), (0, LANES - n_gates)))
    b_gates = jnp.pad(b_gates0.astype(F32), (0, LANES - n_gates)).reshape(1, LANES)
    proj, gates, w_gu0_b, w_down0_b, w_out0_b = norm_proj(
        xf, row(norm_mix0), w_in, w_gates, n=n_main,
        casts=(w_gu0, w_down0, w_out0), cast_rows=(32, 128, 32))
    cos, sin = _rotary_tables(seq)
    y, w_qkv1_b, w_out1_b, w_gu1_b = mixer0(
        proj.reshape(bsz, seq, n_main), gates.reshape(bsz, seq, LANES), b_gates,
        w_conv0.astype(F32), row(g_ml0), row(g_ret0), cos, sin,
        casts=(w_qkv1, w_out1, w_gu1), cast_rows=(16, 16, 16))
    xf = proj_residual(y.reshape(m, d), w_out0_b, xf)
    xf, w_down1_b = ffn_residual(xf, row(norm_ffn0), w_gu0_b, w_down0_b, row(final_norm), final_norm=False,
                                 casts=(w_down1,), cast_rows=(32,))

    (qkv,) = norm_proj(xf, row(norm_mix1), w_qkv1_b)
    o = stick_breaking(qkv.reshape(bsz, seq, 3 * d))
    xf = proj_residual(o.reshape(m, d), w_out1_b, xf)
    (xf,) = ffn_residual(xf, row(norm_ffn1), w_gu1_b, w_down1_b, row(final_norm), final_norm=True)
    return xf.reshape(bsz, seq, d)
```

```python
import functools
import math

import numpy as np
import jax
import jax.numpy as jnp
from jax import lax
from jax.experimental import pallas as pl
from jax.experimental.pallas import tpu as pltpu

F32 = jnp.float32
BF16 = jnp.bfloat16

D_MODEL = 2048
GROUP_WIDTH = 1024
ML_HEADS = 4
RET_HEADS = 4
HEAD_DIM0 = 256
SB_HEADS = 16
SB_HEAD_DIM = 128
CHUNK = 128
CONV_WIDTH = 4
FFN_HIDDEN = 5632
ROPE_BASE = 10000.0
EPS = 1e-6
LANES = 128
TAIL_ROWS = 16
NEG_BIG = -1e30

VMEM_LIMIT = 60 * 1024 * 1024


def _params(semantics):
    return pltpu.CompilerParams(dimension_semantics=semantics, vmem_limit_bytes=VMEM_LIMIT)


def _rms_norm_rows(x, g):
    return x * lax.rsqrt(jnp.mean(x * x, axis=-1, keepdims=True) + EPS) * g


def _dot(a, b):
    return jnp.dot(a, b, preferred_element_type=F32)


def _dot_nt(a, b):
    return lax.dot_general(a, b, (((1,), (1,)), ((), ())), preferred_element_type=F32)


def _cast_specs(weights, block_rows, step_of):
    in_specs, out_shapes = [], []
    for w, rows in zip(weights, block_rows):
        blocks, rem = divmod(w.shape[0], rows)
        assert rem == 0, (w.shape, rows)
        spec = pl.BlockSpec((rows, w.shape[1]),
                            lambda *ids, last=blocks - 1: (jnp.minimum(step_of(*ids), last), 0))
        in_specs.append(spec)
        out_shapes.append(jax.ShapeDtypeStruct(w.shape, BF16))
    return in_specs, out_shapes


def _cast_blocks(src_refs, dst_refs):
    for src, dst in zip(src_refs, dst_refs):
        dst[...] = src[...].astype(BF16)


def _norm_proj_kernel(*refs, with_gates, n_cast):
    n_in = 3 + with_gates
    x_ref, g_ref, w_ref = refs[:3]
    cast_src = refs[n_in:n_in + n_cast]
    outs = refs[n_in + n_cast:-1]
    o_ref = outs[0]
    cast_dst = outs[1 + with_gates:]
    xn_ref = refs[-1]

    @pl.when(pl.program_id(1) == 0)
    def _():
        xn = _rms_norm_rows(x_ref[...], g_ref[...]).astype(BF16)
        xn_ref[...] = xn
        if with_gates:
            outs[1][...] = _dot(xn, refs[3][...])

    o_ref[...] = _dot(xn_ref[...], w_ref[...]).astype(o_ref.dtype)
    _cast_blocks(cast_src, cast_dst)


def norm_proj(x, g, w, w_gates=None, *, n=None, casts=(), cast_rows=(), tm=1024, tn=2048):
    m, d = x.shape
    n = w.shape[1] if n is None else n
    nj = n // tn
    grid = (m // tm, nj)
    with_gates = w_gates is not None
    cast_specs, cast_shapes = _cast_specs(casts, cast_rows, lambda i, j: i * nj + j)
    in_specs = [
        pl.BlockSpec((tm, d), lambda i, j: (i, 0)),
        pl.BlockSpec((1, d), lambda i, j: (0, 0)),
        pl.BlockSpec((d, tn), lambda i, j: (0, j)),
    ]
    out_specs = [pl.BlockSpec((tm, tn), lambda i, j: (i, j))]
    out_shapes = [jax.ShapeDtypeStruct((m, n), BF16)]
    operands = [x, g, w]
    if with_gates:
        ng = w_gates.shape[1]
        in_specs.append(pl.BlockSpec((d, ng), lambda i, j: (0, 0)))
        out_specs.append(pl.BlockSpec((tm, ng), lambda i, j: (i, 0)))
        out_shapes.append(jax.ShapeDtypeStruct((m, ng), F32))
        operands.append(w_gates)
    return pl.pallas_call(
        functools.partial(_norm_proj_kernel, with_gates=with_gates, n_cast=len(casts)),
        grid=grid,
        in_specs=in_specs + cast_specs,
        out_specs=out_specs + cast_specs,
        out_shape=out_shapes + cast_shapes,
        scratch_shapes=[pltpu.VMEM((tm, d), BF16)],
        compiler_params=_params(("arbitrary", "arbitrary")),
        name="norm_proj_gates" if with_gates else "norm_proj",
    )(*operands, *casts)


def _proj_residual_kernel(a_ref, w_ref, r_ref, o_ref):
    o_ref[...] = r_ref[...] + _dot(a_ref[...], w_ref[...])


def proj_residual(a, w, r, *, tm=512, tn=2048):
    m, k = a.shape
    n = w.shape[1]
    return pl.pallas_call(
        _proj_residual_kernel,
        grid=(m // tm, n // tn),
        in_specs=[
            pl.BlockSpec((tm, k), lambda i, j: (i, 0)),
            pl.BlockSpec((k, tn), lambda i, j: (0, j)),
            pl.BlockSpec((tm, tn), lambda i, j: (i, j)),
        ],
        out_specs=pl.BlockSpec((tm, tn), lambda i, j: (i, j)),
        out_shape=jax.ShapeDtypeStruct((m, n), F32),
        compiler_params=_params(("parallel", "arbitrary")),
        name="proj_residual",
    )(a, w, r)


def _ffn_kernel(*refs, final_norm, n_cast):
    x_ref, g_ref, wg_ref, wu_ref, wd_ref, gf_ref = refs[:6]
    cast_src = refs[6:6 + n_cast]
    o_ref = refs[6 + n_cast]
    cast_dst = refs[7 + n_cast:-1]
    xn_ref = refs[-1]
    f = pl.program_id(1)

    @pl.when(f == 0)
    def _():
        x = x_ref[...]
        xn_ref[...] = _rms_norm_rows(x, g_ref[...]).astype(BF16)
        o_ref[...] = x

    xn = xn_ref[...]
    gate = _dot(xn, wg_ref[...])
    up = _dot(xn, wu_ref[...])
    act = (gate * jax.nn.sigmoid(gate) * up).astype(BF16)
    o_ref[...] += _dot(act, wd_ref[...])

    if final_norm:
        @pl.when(f == pl.num_programs(1) - 1)
        def _():
            o_ref[...] = _rms_norm_rows(o_ref[...], gf_ref[...])

    _cast_blocks(cast_src, cast_dst)


def ffn_residual(x, g, w_gu, w_down, g_final, *, final_norm, casts=(), cast_rows=(), tm=1024, tf=512):
    m, d = x.shape
    hidden = w_down.shape[0]
    nf = hidden // tf
    row_spec = pl.BlockSpec((1, d), lambda i, f: (0, 0))
    cast_specs, cast_shapes = _cast_specs(casts, cast_rows, lambda i, f: i * nf + f)
    return pl.pallas_call(
        functools.partial(_ffn_kernel, final_norm=final_norm, n_cast=len(casts)),
        grid=(m // tm, nf),
        in_specs=[
            pl.BlockSpec((tm, d), lambda i, f: (i, 0)),
            row_spec,
            pl.BlockSpec((d, tf), lambda i, f: (0, f)),
            pl.BlockSpec((d, tf), lambda i, f: (0, f + nf)),
            pl.BlockSpec((tf, d), lambda i, f: (f, 0)),
            row_spec,
        ] + cast_specs,
        out_specs=[pl.BlockSpec((tm, d), lambda i, f: (i, 0))] + cast_specs,
        out_shape=[jax.ShapeDtypeStruct((m, d), F32)] + cast_shapes,
        scratch_shapes=[pltpu.VMEM((tm, d), BF16)],
        compiler_params=_params(("arbitrary", "arbitrary")),
        name="ffn_residual",
    )(x, g, w_gu, w_gu, w_down, g_final, *casts)


def _shift_matrix(length):
    rows = (CONV_WIDTH - 1) * length
    r = lax.broadcasted_iota(jnp.int32, (rows, 2 * length), 0)
    c = lax.broadcasted_iota(jnp.int32, (rows, 2 * length), 1)
    sel = jnp.zeros((rows, 2 * length), jnp.bool_)
    for s in range(1, CONV_WIDTH):
        in_band = jnp.logical_and(r >= (s - 1) * length, r < s * length)
        sel = jnp.logical_or(sel, jnp.logical_and(in_band, c == r - (s - 1) * length + length - s))
    return jnp.where(sel, 1.0, 0.0).astype(BF16)


def _conv_silu(x_b, tail_b, w, shift):
    length, ch = x_b.shape
    pad = jnp.zeros((length - tail_b.shape[0], ch), BF16)
    shifted = _dot(shift, jnp.concatenate([pad, tail_b, x_b], axis=0))
    acc = x_b.astype(F32) * w[CONV_WIDTH - 1:CONV_WIDTH]
    for s in range(1, CONV_WIDTH):
        acc = acc + shifted[(s - 1) * length:s * length] * w[CONV_WIDTH - 1 - s:CONV_WIDTH - s]
    return acc * jax.nn.sigmoid(acc)


def _log_sigmoid(x):
    return jnp.minimum(x, 0.0) - jnp.log1p(jnp.exp(-jnp.abs(x)))


def _mixer0_kernel(*refs, n_cast):
    proj_ref, gates_ref, bg_ref, wconv_ref, gml_ref, gret_ref, cos_ref, sin_ref = refs[:8]
    cast_src = refs[8:8 + n_cast]
    y_ref = refs[8 + n_cast]
    cast_dst = refs[9 + n_cast:9 + 2 * n_cast]
    c_ref, n_ref, m_ref, r_ref, tail_ref = refs[9 + 2 * n_cast:]
    L, HD, GW = CHUNK, HEAD_DIM0, GROUP_WIDTH
    _cast_blocks(cast_src, cast_dst)

    @pl.when(pl.program_id(1) == 0)
    def _():
        c_ref[...] = jnp.zeros_like(c_ref)
        n_ref[...] = jnp.zeros_like(n_ref)
        m_ref[...] = jnp.zeros_like(m_ref)
        r_ref[...] = jnp.zeros_like(r_ref)
        tail_ref[...] = jnp.zeros_like(tail_ref)

    row = lax.broadcasted_iota(jnp.int32, (L, L), 0)
    col = lax.broadcasted_iota(jnp.int32, (L, L), 1)
    causal = col <= row
    diff = (row - col).astype(F32)

    gates = gates_ref[...] + bg_ref[...]
    gates_t = gates.T

    cos = cos_ref[...]
    sin = sin_ref[...]
    half = HD // 2
    pos_col = lax.broadcasted_iota(jnp.int32, (L, 1), 0).astype(F32)
    shift = _shift_matrix(L)

    def rotary(t):
        t1, t2 = t[:, :half], t[:, half:]
        return jnp.concatenate([t1 * cos - t2 * sin, t1 * sin + t2 * cos], axis=1)


    def ml_operands(h):
        qs = slice(h * HD, (h + 1) * HD)
        ks = slice(GW + h * HD, GW + (h + 1) * HD)
        q_raw = proj_ref[:, qs]
        k_raw = proj_ref[:, ks]
        q = _conv_silu(q_raw, tail_ref[:, qs], wconv_ref[:, qs], shift) * (HD ** -0.5)
        k = _conv_silu(k_raw, tail_ref[:, ks], wconv_ref[:, ks], shift)
        tail_ref[:, qs] = q_raw[L - TAIL_ROWS:]
        tail_ref[:, ks] = k_raw[L - TAIL_ROWS:]

        ig_col = gates[:, h:h + 1]
        lf_col = _log_sigmoid(gates[:, ML_HEADS + h:ML_HEADS + h + 1])
        ig_row = gates_t[h:h + 1, :]
        lf_row = _log_sigmoid(gates_t[ML_HEADS + h:ML_HEADS + h + 1, :])
        b_col = jnp.sum(jnp.where(causal, lf_row, 0.0), axis=1, keepdims=True)
        b_row = jnp.sum(jnp.where(row <= col, lf_col, 0.0), axis=0, keepdims=True)
        g_tot = jnp.sum(lf_row, axis=1, keepdims=True)
        a_col = g_tot - b_col + ig_col
        m_loc = jnp.max(a_col, axis=0, keepdims=True)
        w_end = jnp.exp(a_col - m_loc)

        m_prev = m_ref[h]
        log_d = jnp.where(causal, b_col - b_row + ig_row, NEG_BIG)
        m_inter = b_col + m_prev
        m_t = jnp.maximum(m_inter, jnp.max(log_d, axis=1, keepdims=True))
        m_new = jnp.maximum(g_tot + m_prev, m_loc)
        kw = k * w_end
        return dict(
            h=h, q=q, q_b=q.astype(BF16), k_b=k.astype(BF16), kw_t=kw.T.astype(BF16),
            kw_sum=jnp.sum(kw, axis=0, keepdims=True), decay=jnp.exp(log_d - m_t),
            s_inter=jnp.exp(m_inter - m_t), floor=jnp.exp(-m_t), m_new=m_new,
            s_old=jnp.exp(g_tot + m_prev - m_new), s_new=jnp.exp(m_loc - m_new))

    def ml_first(p):
        h = p["h"]
        v = proj_ref[:, 2 * GW + h * HD:2 * GW + (h + 1) * HD]
        p["qk"] = _dot_nt(p["q_b"], p["k_b"])
        p["qc"] = _dot(p["q_b"], c_ref[h].astype(BF16))
        p["kv"] = _dot(p["kw_t"], v)

    def ml_second(p):
        h = p["h"]
        qs = slice(h * HD, (h + 1) * HD)
        v = proj_ref[:, 2 * GW + h * HD:2 * GW + (h + 1) * HD]
        o_gate = proj_ref[:, 3 * GW + h * HD:3 * GW + (h + 1) * HD].astype(F32)
        n_prev = n_ref[h]
        w = p["qk"] * p["decay"]
        num = _dot(w.astype(BF16), v) + p["s_inter"] * p["qc"]
        den = jnp.sum(w, axis=1, keepdims=True) + p["s_inter"] * jnp.sum(p["q"] * n_prev, axis=1, keepdims=True)
        hid = num / jnp.maximum(jnp.abs(den), p["floor"])
        c_ref[h] = p["s_old"] * c_ref[h] + p["s_new"] * p["kv"]
        n_ref[h] = p["s_old"] * n_prev + p["s_new"] * p["kw_sum"]
        m_ref[h] = p["m_new"]
        hid = hid * lax.rsqrt(jnp.mean(hid * hid, axis=1, keepdims=True) + EPS) * gml_ref[:, qs]
        y_ref[:, qs] = (hid * jax.nn.sigmoid(o_gate)).astype(y_ref.dtype)

    def ret_operands(h):
        base = 4 * GW + h * HD
        log_gamma = float(np.log(np.float32(1.0) - np.float32(2.0) ** np.float32(-5.0 - 2.0 * h)))
        q = rotary(proj_ref[:, base:base + HD].astype(F32))
        k = rotary(proj_ref[:, base + GW:base + GW + HD].astype(F32)) * (HD ** -0.5)
        decay_q = jnp.exp((pos_col + 1.0) * log_gamma)
        decay_k = jnp.exp((L - 1.0 - pos_col) * log_gamma)
        return dict(
            h=h, q_b=q.astype(BF16), k_b=k.astype(BF16), qd_b=(q * decay_q).astype(BF16),
            kd_t=(k * decay_k).T.astype(BF16),
            decay=jnp.where(causal, jnp.exp(jnp.maximum(diff, 0.0) * log_gamma), 0.0),
            decay_chunk=math.exp(L * log_gamma))

    def ret_first(p):
        h = p["h"]
        v = proj_ref[:, 6 * GW + h * HD:6 * GW + (h + 1) * HD]
        p["qk"] = _dot_nt(p["q_b"], p["k_b"])
        p["qr"] = _dot(p["qd_b"], r_ref[h].astype(BF16))
        p["kv"] = _dot(p["kd_t"], v)

    def ret_second(p):
        h = p["h"]
        v = proj_ref[:, 6 * GW + h * HD:6 * GW + (h + 1) * HD]
        r_gate = proj_ref[:, 7 * GW + h * HD:7 * GW + (h + 1) * HD].astype(F32)
        out = _dot((p["qk"] * p["decay"]).astype(BF16), v) + p["qr"]
        r_ref[h] = p["decay_chunk"] * r_ref[h] + p["kv"]
        out = out - jnp.mean(out, axis=1, keepdims=True)
        out = out * lax.rsqrt(jnp.mean(out * out, axis=1, keepdims=True) + EPS) * gret_ref[:, h * HD:(h + 1) * HD]
        y_ref[:, GW + h * HD:GW + (h + 1) * HD] = (out * (r_gate * jax.nn.sigmoid(r_gate))).astype(y_ref.dtype)

    ml = [ml_operands(h) for h in range(ML_HEADS)]
    ret = [ret_operands(h) for h in range(RET_HEADS)]
    for p in ml:
        ml_first(p)
    for p in ret:
        ret_first(p)
    for p in ml:
        ml_second(p)
    for p in ret:
        ret_second(p)


def mixer0(proj, gates, b_gates, w_conv, g_ml, g_ret, cos, sin, *, casts=(), cast_rows=()):
    bsz, seq, width = proj.shape
    hd = HEAD_DIM0
    nc = seq // CHUNK
    cast_specs, cast_shapes = _cast_specs(casts, cast_rows, lambda b, c: b * nc + c)
    return pl.pallas_call(
        functools.partial(_mixer0_kernel, n_cast=len(casts)),
        grid=(bsz, nc),
        in_specs=[
            pl.BlockSpec((None, CHUNK, width), lambda b, c: (b, c, 0)),
            pl.BlockSpec((None, CHUNK, LANES), lambda b, c: (b, c, 0)),
            pl.BlockSpec((1, LANES), lambda b, c: (0, 0)),
            pl.BlockSpec((CONV_WIDTH, 2 * GROUP_WIDTH), lambda b, c: (0, 0)),
            pl.BlockSpec((1, GROUP_WIDTH), lambda b, c: (0, 0)),
            pl.BlockSpec((1, GROUP_WIDTH), lambda b, c: (0, 0)),
            pl.BlockSpec((CHUNK, hd // 2), lambda b, c: (c, 0)),
            pl.BlockSpec((CHUNK, hd // 2), lambda b, c: (c, 0)),
        ] + cast_specs,
        out_specs=[pl.BlockSpec((None, CHUNK, 2 * GROUP_WIDTH), lambda b, c: (b, c, 0))] + cast_specs,
        out_shape=[jax.ShapeDtypeStruct((bsz, seq, 2 * GROUP_WIDTH), BF16)] + cast_shapes,
        scratch_shapes=[
            pltpu.VMEM((ML_HEADS, hd, hd), F32),
            pltpu.VMEM((ML_HEADS, 1, hd), F32),
            pltpu.VMEM((ML_HEADS, 1, 1), F32),
            pltpu.VMEM((RET_HEADS, hd, hd), F32),
            pltpu.VMEM((TAIL_ROWS, 2 * GROUP_WIDTH), BF16),
        ],
        compiler_params=_params(("arbitrary", "arbitrary")),
        name="mixer0",
    )(proj, gates, b_gates, w_conv, g_ml, g_ret, cos, sin, *casts)


SB_BLOCK = 256
SB_GROUP = 8
SB_LOG_ZERO = -110.0
SB_BOUND_SLACK = 1.001


def _sb_kernel(q_ref, k_ref, v_ref, o_ref, kmax_ref):
    T, HD = SB_BLOCK, SB_HEAD_DIM
    qi = pl.program_id(2)
    heads = [slice(g * HD, (g + 1) * HD) for g in range(SB_GROUP)]

    @pl.when(qi == 0)
    def _():
        for g, hs in enumerate(heads):
            kf = k_ref[:, hs].astype(F32)
            ksq = jnp.max(jnp.sum(kf * kf, axis=1, keepdims=True), axis=0, keepdims=True)
            kmax_ref[g] = jnp.sqrt(ksq)

    row = lax.broadcasted_iota(jnp.int32, (T, T), 0)
    col = lax.broadcasted_iota(jnp.int32, (T, T), 1)
    ones_incl = jnp.where(row >= col, 1.0, 0.0).astype(BF16)
    cum_rhs = jnp.concatenate([ones_incl, ones_incl], axis=0)
    strict = col < row

    qs, zmax = [], []
    for g, hs in enumerate(heads):
        q = (q_ref[:, hs].astype(F32) * (HD ** -0.5)).astype(BF16)
        qf = q.astype(F32)
        qnorm = jnp.sqrt(jnp.sum(qf * qf, axis=1, keepdims=True))
        qs.append(q)
        zmax.append(qnorm * kmax_ref[g] * SB_BOUND_SLACK)

    def span(start, nsub, accs, rems, keep):
        zs = [_dot_nt(qs[g], k_ref[pl.ds(start, nsub * T), hs]) for g, hs in enumerate(heads)]
        cums = []
        for g in range(SB_GROUP):
            z = zs[g]
            log_keep = -(jnp.maximum(z, 0.0) + jnp.log(1.0 + jnp.exp(-jnp.abs(z))))
            split = []
            for u in range(nsub):
                part = log_keep[:, u * T:(u + 1) * T]
                if keep[u] is not None:
                    part = jnp.where(keep[u], part, 0.0)
                hi = part.astype(BF16)
                lo = (part - hi.astype(F32)).astype(BF16)
                split.append(jnp.concatenate([hi, lo], axis=1))
            cums.append(_dot(jnp.concatenate(split, axis=0), cum_rhs))
        new_accs, new_rems = [], []
        for g, hs in enumerate(heads):
            rem = rems[g]
            weights = [None] * nsub
            for u in reversed(range(nsub)):
                cum_u = cums[g][u * T:(u + 1) * T]
                a = jnp.exp(zs[g][:, u * T:(u + 1) * T] + cum_u + rem)
                if keep[u] is not None:
                    a = jnp.where(keep[u], a, 0.0)
                weights[u] = a.astype(BF16)
                rem = rem + cum_u[:, 0:1]
            v = v_ref[pl.ds(start, nsub * T), hs]
            new_accs.append(accs[g] + _dot(jnp.concatenate(weights, axis=1), v))
            new_rems.append(rem)
        return tuple(new_accs), tuple(new_rems)

    def live(rems):
        worst = rems[0] + zmax[0]
        for g in range(1, SB_GROUP):
            worst = jnp.maximum(worst, rems[g] + zmax[g])
        return jnp.max(worst) >= SB_LOG_ZERO

    def zeros():
        return (tuple(jnp.zeros((T, HD), F32) for _ in heads), tuple(jnp.zeros((T, 1), F32) for _ in heads))

    def finish(accs):
        for g, hs in enumerate(heads):
            o_ref[:, hs] = accs[g].astype(o_ref.dtype)

    @pl.when(qi == 0)
    def _():
        accs, _ = span(0, 1, *zeros(), [strict])
        finish(accs)

    @pl.when(qi > 0)
    def _():
        accs, rems = span(pl.multiple_of((qi - 1) * T, T), 2, *zeros(), [None, strict])

        def cond(carry):
            return jnp.logical_and(carry[0] >= -1, carry[1])

        def body(carry):
            first, _, accs, rems = carry
            start = pl.multiple_of(jnp.maximum(first, 0) * T, T)
            accs, rems = span(start, 2, accs, rems, [None, first >= 0])
            return first - 2, live(rems), accs, rems

        _, _, accs, _ = lax.while_loop(cond, body, (qi - 3, live(rems), accs, rems))
        finish(accs)


def stick_breaking(qkv):
    bsz, seq, _ = qkv.shape
    T = SB_BLOCK
    width = SB_GROUP * SB_HEAD_DIM
    groups = SB_HEADS // SB_GROUP
    return pl.pallas_call(
        _sb_kernel,
        grid=(bsz, groups, seq // T),
        in_specs=[
            pl.BlockSpec((None, T, width), lambda b, h, i: (b, i, h)),
            pl.BlockSpec((None, seq, width), lambda b, h, i: (b, 0, groups + h)),
            pl.BlockSpec((None, seq, width), lambda b, h, i: (b, 0, 2 * groups + h)),
        ],
        out_specs=pl.BlockSpec((None, T, width), lambda b, h, i: (b, i, h)),
        out_shape=jax.ShapeDtypeStruct((bsz, seq, D_MODEL), BF16),
        scratch_shapes=[pltpu.VMEM((SB_GROUP, 1, 1), F32)],
        compiler_params=_params(("parallel", "parallel", "arbitrary")),
        name="stick_breaking",
    )(qkv, qkv, qkv)


def _rotary_tables(seq):
    half = HEAD_DIM0 // 2
    inv = ROPE_BASE ** (-jnp.arange(0, HEAD_DIM0, 2, dtype=F32) / HEAD_DIM0)
    ang = jnp.arange(seq, dtype=F32)[:, None] * inv[None, :]
    assert ang.shape == (seq, half)
    return jnp.cos(ang), jnp.sin(ang)


def kernel(x, norm_mix0, w_in0, b_gates0, w_conv0, g_ml0, g_ret0, w_out0, norm_ffn0, w_gu0, w_down0,
           norm_mix1, w_qkv1, w_out1, norm_ffn1, w_gu1, w_down1, final_norm):
    bsz, seq, d = x.shape
    m = bsz * seq
    row = lambda t: t.reshape(1, -1).astype(F32)
    n_main = 8 * GROUP_WIDTH
    n_gates = 2 * ML_HEADS

    xf = x.reshape(m, d)

    w_in = w_in0.astype(BF16)
    w_gates = jnp.pad(w_in[:, n_main:], ((0, 0), (0, LANES - n_gates)))
    b_gates = jnp.pad(b_gates0.astype(F32), (0, LANES - n_gates)).reshape(1, LANES)
    proj, gates, w_gu0_b, w_down0_b, w_out0_b = norm_proj(
        xf, row(norm_mix0), w_in, w_gates, n=n_main,
        casts=(w_gu0, w_down0, w_out0), cast_rows=(32, 128, 32))
    cos, sin = _rotary_tables(seq)
    y, w_qkv1_b, w_out1_b, w_gu1_b = mixer0(
        proj.reshape(bsz, seq, n_main), gates.reshape(bsz, seq, LANES), b_gates,
        w_conv0.astype(F32), row(g_ml0), row(g_ret0), cos, sin,
        casts=(w_qkv1, w_out1, w_gu1), cast_rows=(16, 16, 16))
    xf = proj_residual(y.reshape(m, d), w_out0_b, xf)
    xf, w_down1_b = ffn_residual(xf, row(norm_ffn0), w_gu0_b, w_down0_b, row(final_norm), final_norm=False,
                                 casts=(w_down1,), cast_rows=(32,))

    (qkv,) = norm_proj(xf, row(norm_mix1), w_qkv1_b)
    o = stick_breaking(qkv.reshape(bsz, seq, 3 * d))
    xf = proj_residual(o.reshape(m, d), w_out1_b, xf)
    (xf,) = ffn_residual(xf, row(norm_ffn1), w_gu1_b, w_down1_b, row(final_norm), final_norm=True)
    return xf.reshape(bsz, seq, d)
```
